```python
import math
import jax, jax.numpy as jnp
from jax import lax
import numpy as np

D_MODEL = 4096
BATCH = 4
SEQ = 2048
DEPTH = 4
DEC_BATCH = 128
DEC_SEQ = 1
PAST_LEN = 8192
PAGE_SIZE = 128

N_EVEN = (DEPTH + 1) // 2
N_ODD = DEPTH // 2
HEAD_DIM = 128
ROT_DIM = HEAD_DIM // 4
ROPE_THETA = 500000.0
RMS_EPS = 1e-6

MLA_HEADS = D_MODEL // (2 * HEAD_DIM)
MLA_Q_LORA = 768
MLA_KV_LORA = 256
MLA_NOPE = 128
MLA_ROPE = 64
MLA_V = 128

MOBA_HEADS = D_MODEL // (2 * HEAD_DIM)
MOBA_KV_HEADS = 2
MOBA_BLOCK = 256
MOBA_TOPK = 3

DSA_HEADS = D_MODEL // HEAD_DIM
IDX_HEADS = 16
IDX_DIM = 64
IDX_ROT = IDX_DIM // 4
DSA_TOPK = 256
IDX_WEIGHT_SCALE = (IDX_HEADS ** -0.5) * (IDX_DIM ** -0.5)

D_FF = 2 * D_MODEL
CONV_W = 3

Q_BLOCK = 128
MOBA_Q_BLOCK = 16
NEG = -1e30

SIZES_AB = (MLA_Q_LORA, MLA_KV_LORA, MLA_ROPE, MOBA_HEADS * HEAD_DIM,
            MOBA_KV_HEADS * HEAD_DIM, MOBA_KV_HEADS * HEAD_DIM)
E_IN_AB = MLA_Q_LORA + MLA_KV_LORA + MLA_ROPE + (MOBA_HEADS + 2 * MOBA_KV_HEADS) * HEAD_DIM
E_MIX_AB = MLA_HEADS * MLA_V + MOBA_HEADS * HEAD_DIM
SIZES_C = (DSA_HEADS * HEAD_DIM, HEAD_DIM, HEAD_DIM, IDX_HEADS * IDX_DIM, IDX_DIM, IDX_HEADS)
E_IN_C = (DSA_HEADS + 2) * HEAD_DIM + IDX_HEADS * IDX_DIM + IDX_DIM + IDX_HEADS
E_MIX_C = DSA_HEADS * HEAD_DIM

kernel_name = "hybrid_mla_moba_dsa_convffn_step"


def rmsnorm(x, g):
    xf = x.astype(jnp.float32)
    y = xf * lax.rsqrt(jnp.mean(xf * xf, axis=-1, keepdims=True) + RMS_EPS)
    return (y * g.astype(jnp.float32)).astype(x.dtype)


def rope(x, pos, rot_dim):
    half = rot_dim // 2
    inv = ROPE_THETA ** (-jnp.arange(half, dtype=jnp.float32) * 2.0 / rot_dim)
    ang = pos.astype(jnp.float32)[:, None] * inv[None, :]
    ang = ang.reshape((ang.shape[0],) + (1,) * (x.ndim - 3) + (half,))
    cos = jnp.cos(ang).astype(x.dtype)
    sin = jnp.sin(ang).astype(x.dtype)
    x1 = x[..., :half]
    x2 = x[..., half:rot_dim]
    return jnp.concatenate([x1 * cos - x2 * sin, x1 * sin + x2 * cos, x[..., rot_dim:]], axis=-1)


def masked_softmax(s, mask):
    return jax.nn.softmax(jnp.where(mask, s.astype(jnp.float32), NEG), axis=-1)


def split_cols(x, sizes):
    out, off = [], 0
    for s in sizes:
        out.append(x[..., off:off + s])
        off += s
    return out


def over_query_blocks(fn, block, q_arrays, q_pos):
    n_q = q_pos.shape[0]
    if n_q <= block or n_q % block:
        return fn(*q_arrays, q_pos)
    nb = n_q // block
    xs = tuple(a.reshape((a.shape[0], nb, block) + a.shape[2:]).swapaxes(0, 1) for a in q_arrays)
    out = lax.map(lambda args: fn(*args[0], args[1]), (xs, q_pos.reshape(nb, block)))
    out = out.swapaxes(0, 1)
    return out.reshape((out.shape[0], n_q) + out.shape[3:])


def gather_pages(pool, page_table):
    g = pool[page_table]
    return g.reshape((g.shape[0], g.shape[1] * g.shape[2]) + g.shape[3:])


def mla_attend(q_nope, q_rope, c_kv, k_rope, w_uk, w_uv, q_pos):
    scale = 1.0 / math.sqrt(MLA_NOPE + MLA_ROPE)
    k_pos = jnp.arange(c_kv.shape[1])
    q_lat = jnp.einsum('bqhn,chn->bqhc', q_nope, w_uk)

    def block(ql, qr, qp):
        s = jnp.einsum('bqhc,bkc->bhqk', ql, c_kv) + jnp.einsum('bqhr,bkr->bhqk', qr, k_rope)
        p = masked_softmax(s * scale, k_pos[None, :] <= qp[:, None])
        return jnp.einsum('bhqk,bkc->bqhc', p.astype(c_kv.dtype), c_kv)

    o_lat = over_query_blocks(block, Q_BLOCK, (q_lat, q_rope), q_pos)
    return jnp.einsum('bqhc,chv->bqhv', o_lat, w_uv)


def moba_attend(q, k, v, q_pos):
    B, n_k = k.shape[:2]
    nb = -(-n_k // MOBA_BLOCK)
    pad = nb * MOBA_BLOCK - n_k
    kb = jnp.pad(k, ((0, 0), (0, pad), (0, 0), (0, 0))).reshape(B, nb, MOBA_BLOCK, MOBA_KV_HEADS, HEAD_DIM)
    vb = jnp.pad(v, ((0, 0), (0, pad), (0, 0), (0, 0))).reshape(B, nb, MOBA_BLOCK, MOBA_KV_HEADS, HEAD_DIM)
    k_mean = jnp.mean(kb.astype(jnp.float32), axis=2).astype(k.dtype)
    kb_h = kb.transpose(0, 3, 1, 2, 4)
    vb_h = vb.transpose(0, 3, 1, 2, 4)
    group = MOBA_HEADS // MOBA_KV_HEADS
    n_top = min(MOBA_TOPK, nb)
    scale = HEAD_DIM ** -0.5
    b_idx = jnp.arange(B)[:, None, None, None]
    h_idx = (jnp.arange(MOBA_HEADS) // group)[None, None, :, None]

    def block(qb, qp):
        tq = qp.shape[0]
        own = qp // MOBA_BLOCK
        gate = jnp.einsum('bqkgd,bnkd->bqkgn', qb.reshape(B, tq, MOBA_KV_HEADS, group, HEAD_DIM), k_mean)
        gate = gate.reshape(B, tq, MOBA_HEADS, nb).astype(jnp.float32)
        fully_past = jnp.arange(nb)[None, :] < own[:, None]
        gate = jnp.where(fully_past[None, :, None, :], gate, -jnp.inf)
        _, sel = lax.top_k(gate, n_top)
        own_b = jnp.broadcast_to(own[None, :, None, None], (B, tq, MOBA_HEADS, 1)).astype(sel.dtype)
        blocks = jnp.concatenate([sel, own_b], axis=-1)
        blk_ok = jnp.concatenate([sel < own[None, :, None, None], jnp.ones_like(own_b, dtype=bool)], axis=-1)
        kg = kb_h[b_idx, h_idx, blocks]
        vg = vb_h[b_idx, h_idx, blocks]
        kpos = blocks[..., None] * MOBA_BLOCK + jnp.arange(MOBA_BLOCK)
        mask = blk_ok[..., None] & (kpos <= qp[None, :, None, None, None])
        s = jnp.einsum('bqhd,bqhnsd->bqhns', qb, kg) * scale
        p = masked_softmax(s.reshape(B, tq, MOBA_HEADS, -1), mask.reshape(B, tq, MOBA_HEADS, -1))
        return jnp.einsum('bqhm,bqhmd->bqhd', p.astype(v.dtype), vg.reshape(B, tq, MOBA_HEADS, -1, HEAD_DIM))

    return over_query_blocks(block, MOBA_Q_BLOCK, (q,), q_pos)


def dsa_attend(q, k, v, q_idx, k_idx, w_idx, q_pos):
    B, n_k = k.shape[:2]
    n_keep = min(DSA_TOPK, n_k // 4)
    k_pos = jnp.arange(n_k)
    b_idx = jnp.arange(B)[:, None, None]
    scale = HEAD_DIM ** -0.5

    def block(qb, qib, wb, qp):
        logits = jax.nn.relu(jnp.einsum('bqhd,bkd->bqhk', qib, k_idx))
        score = jnp.einsum('bqh,bqhk->bqk', wb, logits).astype(jnp.float32)
        score = jnp.where(k_pos[None, None, :] <= qp[None, :, None], score, -jnp.inf)
        _, sel = lax.top_k(score, n_keep)
        ok = sel <= qp[None, :, None]
        kg = k[b_idx, sel]
        vg = v[b_idx, sel]
        s = jnp.einsum('bqhd,bqnd->bqhn', qb, kg) * scale
        p = masked_softmax(s, ok[:, :, None, :])
        return jnp.einsum('bqhn,bqnd->bqhd', p.astype(v.dtype), vg)

    return over_query_blocks(block, Q_BLOCK, (q, q_idx, w_idx), q_pos)


def even_mixer(h, pos, past, w_in, q_norm, w_q_up, kv_norm, w_uk, w_uv, w_out):
    B, T, _ = h.shape
    cq, ckv, kr, mq, mk, mv = split_cols(h @ w_in, SIZES_AB)
    q = (rmsnorm(cq, q_norm) @ w_q_up).reshape(B, T, MLA_HEADS, MLA_NOPE + MLA_ROPE)
    q_nope = q[..., :MLA_NOPE]
    q_rope = rope(q[..., MLA_NOPE:], pos, MLA_ROPE)
    c_kv = rmsnorm(ckv, kv_norm)
    k_rope = rope(kr, pos, MLA_ROPE)
    mq = rope(mq.reshape(B, T, MOBA_HEADS, HEAD_DIM), pos, ROT_DIM)
    mk = rope(mk.reshape(B, T, MOBA_KV_HEADS, HEAD_DIM), pos, ROT_DIM)
    mv = mv.reshape(B, T, MOBA_KV_HEADS, HEAD_DIM)
    new_rows = (c_kv, k_rope, mk, mv)
    if past is None:
        c_all, kr_all, mk_all, mv_all = new_rows
    else:
        c_all, kr_all, mk_all, mv_all = [jnp.concatenate([p_, n_.astype(p_.dtype)], axis=1)
                                         for p_, n_ in zip(past, new_rows)]
    o_mla = mla_attend(q_nope, q_rope, c_all, kr_all, w_uk, w_uv, pos)
    o_moba = moba_attend(mq, mk_all, mv_all, pos)
    o = jnp.concatenate([o_mla.reshape(B, T, -1), o_moba.reshape(B, T, -1)], axis=-1)
    return o @ w_out, new_rows


def odd_mixer(h, pos, past, w_in, w_out):
    B, T, _ = h.shape
    q, k, v, qi, ki, wi = split_cols(h @ w_in, SIZES_C)
    q = rope(q.reshape(B, T, DSA_HEADS, HEAD_DIM), pos, ROT_DIM)
    k = rope(k, pos, ROT_DIM)
    qi = rope(qi.reshape(B, T, IDX_HEADS, IDX_DIM), pos, IDX_ROT)
    ki = rope(ki, pos, IDX_ROT)
    wi = wi * IDX_WEIGHT_SCALE
    new_rows = (k, v, ki)
    if past is None:
        k_all, v_all, ki_all = new_rows
    else:
        k_all, v_all, ki_all = [jnp.concatenate([p_, n_.astype(p_.dtype)], axis=1)
                                for p_, n_ in zip(past, new_rows)]
    o = dsa_attend(q, k_all, v_all, qi, ki_all, wi, pos)
    return o.reshape(B, T, -1) @ w_out, new_rows


def conv_ffn(h, conv_state, w_in, conv_w, conv_b, w_out):
    B, T, _ = h.shape
    u, g = split_cols(h @ w_in, (D_FF, D_FF))
    if conv_state is None:
        prev = jnp.zeros((B, CONV_W - 1, D_FF), u.dtype)
    else:
        prev = conv_state.astype(u.dtype)
    u_ext = jnp.concatenate([prev, u], axis=1)
    u_conv = conv_b
    for j in range(CONV_W):
        u_conv = u_conv + conv_w[j] * u_ext[:, j:j + T]
    y = (jax.nn.silu(u_conv) * g) @ w_out
    return y, u_ext[:, T:]


def setup_inputs(seed: int = 0) -> dict:
    key = jax.random.key(seed)
    ks = jax.random.split(key, 32)

    def nrm(k, shape, scale=1.0):
        return jax.random.normal(k, shape, jnp.float32) * scale

    n_pages = PAST_LEN // PAGE_SIZE
    n_used = DEC_BATCH * n_pages
    pool_pages = n_used + (n_used + 3) // 4
    page_table = jax.random.permutation(ks[0], pool_pages)[:n_used].reshape(DEC_BATCH, n_pages).astype(jnp.int32)
    return {
        "x_prompt": nrm(ks[1], (BATCH, SEQ, D_MODEL)),
        "x_sample": nrm(ks[2], (DEC_BATCH, DEC_SEQ, D_MODEL)),
        "cache_mla_latent": nrm(ks[3], (N_EVEN, pool_pages, PAGE_SIZE, MLA_KV_LORA)),
        "cache_mla_krope": nrm(ks[4], (N_EVEN, pool_pages, PAGE_SIZE, MLA_ROPE)),
        "cache_moba_k": nrm(ks[5], (N_EVEN, pool_pages, PAGE_SIZE, MOBA_KV_HEADS, HEAD_DIM)),
        "cache_moba_v": nrm(ks[6], (N_EVEN, pool_pages, PAGE_SIZE, MOBA_KV_HEADS, HEAD_DIM)),
        "cache_dsa_k": nrm(ks[7], (N_ODD, pool_pages, PAGE_SIZE, HEAD_DIM)),
        "cache_dsa_v": nrm(ks[8], (N_ODD, pool_pages, PAGE_SIZE, HEAD_DIM)),
        "cache_dsa_kidx": nrm(ks[9], (N_ODD, pool_pages, PAGE_SIZE, IDX_DIM)),
        "state_ffn_conv": nrm(ks[10], (DEPTH, DEC_BATCH, CONV_W - 1, D_FF)),
        "page_table": page_table,
        "norm_attn": 1.0 + nrm(ks[11], (DEPTH, D_MODEL), 0.05),
        "norm_ffn": 1.0 + nrm(ks[12], (DEPTH, D_MODEL), 0.05),
        "norm_final": 1.0 + nrm(ks[13], (D_MODEL,), 0.05),
        "w_in_ab": nrm(ks[14], (N_EVEN, D_MODEL, E_IN_AB), D_MODEL ** -0.5),
        "mla_q_norm": 1.0 + nrm(ks[15], (N_EVEN, MLA_Q_LORA), 0.05),
        "mla_w_q_up": nrm(ks[16], (N_EVEN, MLA_Q_LORA, MLA_HEADS * (MLA_NOPE + MLA_ROPE)), MLA_Q_LORA ** -0.5),
        "mla_kv_norm": 1.0 + nrm(ks[17], (N_EVEN, MLA_KV_LORA), 0.05),
        "mla_w_uk": nrm(ks[18], (N_EVEN, MLA_KV_LORA, MLA_HEADS, MLA_NOPE), MLA_KV_LORA ** -0.5),
        "mla_w_uv": nrm(ks[19], (N_EVEN, MLA_KV_LORA, MLA_HEADS, MLA_V), MLA_KV_LORA ** -0.5),
        "w_out_ab": nrm(ks[20], (N_EVEN, E_MIX_AB, D_MODEL), E_MIX_AB ** -0.5),
        "w_in_c": nrm(ks[21], (N_ODD, D_MODEL, E_IN_C), D_MODEL ** -0.5),
        "w_out_c": nrm(ks[22], (N_ODD, E_MIX_C, D_MODEL), E_MIX_C ** -0.5),
        "w_ffn_in": nrm(ks[23], (DEPTH, D_MODEL, 2 * D_FF), D_MODEL ** -0.5),
        "conv_w": nrm(ks[24], (DEPTH, CONV_W, D_FF), CONV_W ** -0.5),
        "conv_b": nrm(ks[25], (DEPTH, D_FF), 0.01),
        "w_ffn_out": nrm(ks[26], (DEPTH, D_FF, D_MODEL), D_FF ** -0.5),
    }


def reference(x_prompt, x_sample, cache_mla_latent, cache_mla_krope, cache_moba_k, cache_moba_v,
              cache_dsa_k, cache_dsa_v, cache_dsa_kidx, state_ffn_conv, page_table,
              norm_attn, norm_ffn, norm_final, w_in_ab, mla_q_norm, mla_w_q_up, mla_kv_norm,
              mla_w_uk, mla_w_uv, w_out_ab, w_in_c, w_out_c, w_ffn_in, conv_w, conv_b, w_ffn_out):
    pos_p = jnp.arange(SEQ, dtype=jnp.int32)
    pos_s = PAST_LEN + jnp.arange(DEC_SEQ, dtype=jnp.int32)
    h_p, h_s = x_prompt, x_sample
    lat_p, lat_s, kr_p, kr_s, mk_p, mk_s, mv_p, mv_s = [], [], [], [], [], [], [], []
    dk_p, dk_s, dv_p, dv_s, dki_p, dki_s, cv_p, cv_s = [], [], [], [], [], [], [], []
    for layer in range(DEPTH):
        i = layer // 2
        if layer % 2 == 0:
            ew = (w_in_ab[i], mla_q_norm[i], mla_w_q_up[i], mla_kv_norm[i], mla_w_uk[i], mla_w_uv[i], w_out_ab[i])
            past = (gather_pages(cache_mla_latent[i], page_table), gather_pages(cache_mla_krope[i], page_table),
                    gather_pages(cache_moba_k[i], page_table), gather_pages(cache_moba_v[i], page_table))
            o_p, rows_p = even_mixer(rmsnorm(h_p, norm_attn[layer]), pos_p, None, *ew)
            o_s, rows_s = even_mixer(rmsnorm(h_s, norm_attn[layer]), pos_s, past, *ew)
            lat_p.append(rows_p[0]); kr_p.append(rows_p[1]); mk_p.append(rows_p[2]); mv_p.append(rows_p[3])
            lat_s.append(rows_s[0]); kr_s.append(rows_s[1]); mk_s.append(rows_s[2]); mv_s.append(rows_s[3])
        else:
            past = (gather_pages(cache_dsa_k[i], page_table), gather_pages(cache_dsa_v[i], page_table),
                    gather_pages(cache_dsa_kidx[i], page_table))
            o_p, rows_p = odd_mixer(rmsnorm(h_p, norm_attn[layer]), pos_p, None, w_in_c[i], w_out_c[i])
            o_s, rows_s = odd_mixer(rmsnorm(h_s, norm_attn[layer]), pos_s, past, w_in_c[i], w_out_c[i])
            dk_p.append(rows_p[0]); dv_p.append(rows_p[1]); dki_p.append(rows_p[2])
            dk_s.append(rows_s[0]); dv_s.append(rows_s[1]); dki_s.append(rows_s[2])
        h_p = h_p + o_p
        h_s = h_s + o_s
        f_p, st_p = conv_ffn(rmsnorm(h_p, norm_ffn[layer]), None,
                             w_ffn_in[layer], conv_w[layer], conv_b[layer], w_ffn_out[layer])
        f_s, st_s = conv_ffn(rmsnorm(h_s, norm_ffn[layer]), state_ffn_conv[layer],
                             w_ffn_in[layer], conv_w[layer], conv_b[layer], w_ffn_out[layer])
        h_p = h_p + f_p
        h_s = h_s + f_s
        cv_p.append(st_p)
        cv_s.append(st_s)
    y_prompt = rmsnorm(h_p, norm_final)
    y_sample = rmsnorm(h_s, norm_final)
    return (y_prompt, y_sample,
            jnp.stack(lat_p), jnp.stack(lat_s), jnp.stack(kr_p), jnp.stack(kr_s),
            jnp.stack(mk_p), jnp.stack(mk_s), jnp.stack(mv_p), jnp.stack(mv_s),
            jnp.stack(dk_p), jnp.stack(dk_s), jnp.stack(dv_p), jnp.stack(dv_s),
            jnp.stack(dki_p), jnp.stack(dki_s), jnp.stack(cv_p), jnp.stack(cv_s))
```

```python
import functools
import math

import jax
import jax.numpy as jnp
from jax import lax
from jax.experimental import pallas as pl
from jax.experimental.pallas import tpu as pltpu

HEAD_DIM = 128
ROT_DIM = HEAD_DIM // 4
ROPE_THETA = 500000.0
RMS_EPS = 1e-6
MOBA_BLOCK = 256
MOBA_TOPK = 3
DSA_TOPK = 256
IDX_DIM = 64
NEG = -1e30
M_INIT = -1e29
LANES = 128
SUBLANES = 8
VMEM_LIMIT = 52 * 1024 * 1024
MM_VMEM_BUDGET = 40 * 1024 * 1024
PAGES_PER_STEP = 8

F32 = jnp.float32
BF16 = jnp.bfloat16
NT = (((1,), (1,)), ((), ()))


def _cparams(n_axes):
    return pltpu.CompilerParams(dimension_semantics=("arbitrary",) * n_axes, vmem_limit_bytes=VMEM_LIMIT)


def _tile(n, cands):
    for c in cands:
        if n % c == 0:
            return c
    return n


def _round_up(n, m):
    return -(-n // m) * m


def _dot(a, b):
    return jnp.dot(a, b, preferred_element_type=F32)


def _dot_nt(a, b):
    return lax.dot_general(a, b, NT, preferred_element_type=F32)


def _mm_body(*refs, has_res):
    if has_res:
        x_ref, w_ref, r_ref, o_ref = refs
    else:
        x_ref, w_ref, o_ref = refs
    acc = _dot(x_ref[...], w_ref[...])
    if has_res:
        acc = acc + r_ref[...]
    o_ref[...] = acc.astype(o_ref.dtype)


def _mm_tiles(m, k, n):
    for tm in (1024, 512, 256, 128):
        if m % tm:
            continue
        for tn in (512, 256, 128):
            if n % tn:
                continue
            if 2 * (tm * k * 2 + k * tn * 2 + 2 * tm * tn * 4) <= MM_VMEM_BUDGET:
                return tm, tn
    return _tile(m, (128, 8)), _tile(n, (128,))


def _mm(x, w, res=None, out_dtype=F32):
    m, k = x.shape
    n = w.shape[1]
    tm, tn = _mm_tiles(m, k, n)
    in_specs = [pl.BlockSpec((tm, k), lambda i, j: (i, 0)), pl.BlockSpec((k, tn), lambda i, j: (0, j))]
    args = [x, w]
    if res is not None:
        in_specs.append(pl.BlockSpec((tm, tn), lambda i, j: (i, j)))
        args.append(res)
    return pl.pallas_call(
        functools.partial(_mm_body, has_res=res is not None),
        grid=(m // tm, n // tn),
        in_specs=in_specs,
        out_specs=pl.BlockSpec((tm, tn), lambda i, j: (i, j)),
        out_shape=jax.ShapeDtypeStruct((m, n), out_dtype),
        compiler_params=_cparams(2),
        name="mm",
    )(*args)


def _bmm_body(x_ref, w_ref, o_ref):
    o_ref[...] = _dot(x_ref[...].astype(BF16), w_ref[0]).astype(o_ref.dtype)


def _bmm_heads(x, w):
    r = x.shape[0]
    h, k, n = w.shape
    return pl.pallas_call(
        _bmm_body,
        grid=(h,),
        in_specs=[pl.BlockSpec((r, k), lambda i: (0, i)), pl.BlockSpec((1, k, n), lambda i: (i, 0, 0))],
        out_specs=pl.BlockSpec((r, n), lambda i: (0, i)),
        out_shape=jax.ShapeDtypeStruct((r, h * n), F32),
        compiler_params=_cparams(1),
        name="bmm_heads",
    )(x, w)


def _rms(x, g):
    return x * lax.rsqrt(jnp.mean(x * x, axis=-1, keepdims=True) + RMS_EPS) * g


def _rms_body(x_ref, g_ref, o_ref):
    o_ref[...] = _rms(x_ref[...], g_ref[...]).astype(o_ref.dtype)


def _rmsnorm(x, g, out_dtype):
    m, d = x.shape
    tm = _tile(m, (256, 128, 8))
    return pl.pallas_call(
        _rms_body,
        grid=(m // tm,),
        in_specs=[pl.BlockSpec((tm, d), lambda i: (i, 0)), pl.BlockSpec((1, d), lambda i: (0, 0))],
        out_specs=pl.BlockSpec((tm, d), lambda i: (i, 0)),
        out_shape=jax.ShapeDtypeStruct((m, d), out_dtype),
        compiler_params=_cparams(1),
        name="rmsnorm",
    )(x, g.reshape(1, d))


def _rope_tables(pos, rot_dim, period):
    half = rot_dim // 2
    inv = ROPE_THETA ** (-jnp.arange(half, dtype=F32) * 2.0 / rot_dim)
    ang = pos.astype(F32)[:, None] * inv[None, :]
    cos, sin = jnp.cos(ang), jnp.sin(ang)
    t = pos.shape[0]
    z_half = jnp.zeros((t, half), F32)
    z_rest = jnp.zeros((t, period - rot_dim), F32)
    c = jnp.concatenate([cos, cos, jnp.ones((t, period - rot_dim), F32)], axis=1)
    sm = jnp.concatenate([-sin, z_half, z_rest], axis=1)
    sp = jnp.concatenate([z_half, sin, z_rest], axis=1)
    reps = LANES // period
    return tuple(jnp.tile(a, (1, reps)) for a in (c, sm, sp))


def _rope128(x, tabs, half):
    c, sm, sp = tabs
    return x * c + pltpu.roll(x, LANES - half, 1) * sm + pltpu.roll(x, half, 1) * sp


def _post_even_body(z_ref, qn_ref, kvn_ref, c1, s1m, s1p, c2, s2m, s2p,
                    cqn_ref, ckv_ref, kr_ref, mq_ref, mk_ref, mv_ref, *, offs, rope_dim):
    o_cq, o_ckv, o_mq, o_mk, o_mv, o_kr, _ = offs
    cqn_ref[...] = _rms(z_ref[:, o_cq:o_ckv], qn_ref[...]).astype(cqn_ref.dtype)
    ckv_ref[...] = _rms(z_ref[:, o_ckv:o_mq], kvn_ref[...])
    t1 = (c1[...], s1m[...], s1p[...])
    for g in range((o_mk - o_mq) // LANES):
        mq_ref[:, g * LANES:(g + 1) * LANES] = _rope128(z_ref[:, o_mq + g * LANES:o_mq + (g + 1) * LANES], t1, ROT_DIM // 2)
    for g in range((o_mv - o_mk) // LANES):
        mk_ref[:, g * LANES:(g + 1) * LANES] = _rope128(z_ref[:, o_mk + g * LANES:o_mk + (g + 1) * LANES], t1, ROT_DIM // 2)
    mv_ref[...] = z_ref[:, o_mv:o_kr]
    t2 = (c2[...], s2m[...], s2p[...])
    kr_ref[...] = _rope128(z_ref[:, o_kr:o_kr + LANES], t2, rope_dim // 2)[:, :rope_dim]


def _post_even(z, q_norm, kv_norm, tabs_head, tabs_mla, offs, rope_dim):
    m, nz = z.shape
    o_cq, o_ckv, o_mq, o_mk, o_mv, o_kr, _ = offs
    tm = _tile(m, (256, 128, 8))
    row = lambda w: pl.BlockSpec((tm, w), lambda i: (i, 0))
    full = lambda w: pl.BlockSpec((1, w), lambda i: (0, 0))
    widths = (o_ckv - o_cq, o_mq - o_ckv, rope_dim, o_mk - o_mq, o_mv - o_mk, o_kr - o_mv)
    dtypes = (BF16, F32, F32, F32, F32, F32)
    return pl.pallas_call(
        functools.partial(_post_even_body, offs=offs, rope_dim=rope_dim),
        grid=(m // tm,),
        in_specs=[row(nz), full(widths[0]), full(widths[1])] + [row(LANES)] * 6,
        out_specs=[row(w) for w in widths],
        out_shape=[jax.ShapeDtypeStruct((m, w), d) for w, d in zip(widths, dtypes)],
        compiler_params=_cparams(1),
        name="post_even",
    )(z, q_norm.reshape(1, -1), kv_norm.reshape(1, -1), *tabs_head, *tabs_mla)


def _post_q_body(q_ref, c2, s2m, s2p, qn_ref, qr_ref, *, n_nope, n_heads, rope_dim):
    qn_ref[...] = q_ref[:, :n_nope].astype(qn_ref.dtype)
    t2 = (c2[...], s2m[...], s2p[...])
    per = LANES // rope_dim
    for g in range(n_heads // per):
        r = _rope128(q_ref[:, n_nope + g * LANES:n_nope + (g + 1) * LANES], t2, rope_dim // 2)
        for u in range(per):
            qr_ref[g * per + u] = r[:, u * rope_dim:(u + 1) * rope_dim].astype(qr_ref.dtype)


def _post_q(q, tabs_mla, n_heads, rope_dim):
    m = q.shape[0]
    n_nope = n_heads * HEAD_DIM
    tm = _tile(m, (256, 128, 8))
    return pl.pallas_call(
        functools.partial(_post_q_body, n_nope=n_nope, n_heads=n_heads, rope_dim=rope_dim),
        grid=(m // tm,),
        in_specs=[pl.BlockSpec((tm, q.shape[1]), lambda i: (i, 0))] + [pl.BlockSpec((tm, LANES), lambda i: (i, 0))] * 3,
        out_specs=[pl.BlockSpec((tm, n_nope), lambda i: (i, 0)),
                   pl.BlockSpec((n_heads, tm, rope_dim), lambda i: (0, i, 0))],
        out_shape=[jax.ShapeDtypeStruct((m, n_nope), BF16), jax.ShapeDtypeStruct((n_heads, m, rope_dim), BF16)],
        compiler_params=_cparams(1),
        name="post_q",
    )(q, *tabs_mla)


def _post_odd_body(z_ref, c1, s1m, s1p, c3, s3m, s3p, q_ref, k_ref, v_ref, qi_ref, ki_ref, wi_ref,
                   *, offs, n_idx_heads, w_scale):
    o_q, o_k, o_v, o_qi, o_ki = offs
    t1 = (c1[...], s1m[...], s1p[...])
    t3 = (c3[...], s3m[...], s3p[...])
    for g in range((o_k - o_q) // LANES):
        q_ref[:, g * LANES:(g + 1) * LANES] = _rope128(
            z_ref[:, o_q + g * LANES:o_q + (g + 1) * LANES], t1, ROT_DIM // 2).astype(q_ref.dtype)
    k_ref[...] = _rope128(z_ref[:, o_k:o_v], t1, ROT_DIM // 2)
    v_ref[...] = z_ref[:, o_v:o_qi]
    idx_half = IDX_DIM // 4 // 2
    for g in range((o_ki - o_qi) // LANES):
        qi_ref[:, g * LANES:(g + 1) * LANES] = _rope128(
            z_ref[:, o_qi + g * LANES:o_qi + (g + 1) * LANES], t3, idx_half).astype(qi_ref.dtype)
    tail = z_ref[:, o_ki:o_ki + LANES]
    ki_ref[...] = _rope128(tail, t3, idx_half)[:, :IDX_DIM]
    wi_ref[...] = tail[:, IDX_DIM:IDX_DIM + n_idx_heads] * w_scale


def _post_odd(z, tabs_head, tabs_idx, offs, n_idx_heads):
    m, nz = z.shape
    o_q, o_k, o_v, o_qi, o_ki = offs
    tm = _tile(m, (256, 128, 8))
    row = lambda w: pl.BlockSpec((tm, w), lambda i: (i, 0))
    widths = (o_k - o_q, o_v - o_k, o_qi - o_v, o_ki - o_qi, IDX_DIM, n_idx_heads)
    dtypes = (BF16, F32, F32, BF16, F32, F32)
    w_scale = (n_idx_heads ** -0.5) * (IDX_DIM ** -0.5)
    return pl.pallas_call(
        functools.partial(_post_odd_body, offs=offs, n_idx_heads=n_idx_heads, w_scale=w_scale),
        grid=(m // tm,),
        in_specs=[row(nz)] + [row(LANES)] * 6,
        out_specs=[row(w) for w in widths],
        out_shape=[jax.ShapeDtypeStruct((m, w), d) for w, d in zip(widths, dtypes)],
        compiler_params=_cparams(1),
        name="post_odd",
    )(z, *tabs_head, *tabs_idx)


def _softmax_step(carry, s, v_bf):
    m, l, acc = carry
    m_new = jnp.maximum(m, jnp.max(s, axis=1, keepdims=True))
    a = jnp.exp(m - m_new)
    p = jnp.exp(s - m_new)
    l = a * l + jnp.sum(p, axis=1, keepdims=True)
    acc = a * acc + _dot(p.astype(BF16), v_bf)
    return m_new, l, acc


def _softmax_init(tq, dv):
    return (jnp.full((tq, 1), M_INIT, F32), jnp.zeros((tq, 1), F32), jnp.zeros((tq, dv), F32))


def _causal(s, q0, k0):
    qpos = q0 + lax.broadcasted_iota(jnp.int32, s.shape, 0)
    kpos = k0 + lax.broadcasted_iota(jnp.int32, s.shape, 1)
    return jnp.where(kpos <= qpos, s, NEG)


def _mla_prompt_body(qn_ref, qr_ref, kn_ref, kr_ref, v_ref, o_ref, *, tq, tk, scale):
    q0 = pl.program_id(2) * tq
    qn = qn_ref[...]
    qr = qr_ref[0]

    def step(j, carry):
        k0 = pl.multiple_of(j * tk, tk)
        s = _dot_nt(qn, kn_ref[pl.ds(k0, tk), :]) + _dot_nt(qr, kr_ref[pl.ds(k0, tk), :].astype(BF16))
        s = _causal(s * scale, q0, k0)
        return _softmax_step(carry, s, v_ref[pl.ds(k0, tk), :])

    _, l, acc = lax.fori_loop(0, (q0 + tq + tk - 1) // tk, step, _softmax_init(tq, v_ref.shape[1]))
    o_ref[...] = (acc / l).astype(o_ref.dtype)


def _mla_prompt(qn, qr, kn, kr, v, batch, seq, n_heads, scale):
    tq = tk = _tile(seq, (256, 128))
    nq = seq // tq
    rope_dim = kr.shape[1]
    return pl.pallas_call(
        functools.partial(_mla_prompt_body, tq=tq, tk=tk, scale=scale),
        grid=(batch, n_heads, nq),
        in_specs=[pl.BlockSpec((tq, HEAD_DIM), lambda b, h, i: (b * nq + i, h)),
                  pl.BlockSpec((1, tq, rope_dim), lambda b, h, i: (h, b * nq + i, 0)),
                  pl.BlockSpec((seq, HEAD_DIM), lambda b, h, i: (b, h)),
                  pl.BlockSpec((seq, rope_dim), lambda b, h, i: (b, 0)),
                  pl.BlockSpec((seq, HEAD_DIM), lambda b, h, i: (b, h))],
        out_specs=pl.BlockSpec((tq, HEAD_DIM), lambda b, h, i: (b * nq + i, h)),
        out_shape=jax.ShapeDtypeStruct(qn.shape, BF16),
        compiler_params=_cparams(3),
        name="mla_prompt",
    )(qn, qr, kn, kr, v)


def _topk_lane_mask(gate, n_valid, n_top):
    lane = lax.broadcasted_iota(jnp.int32, gate.shape, 1)
    rank = jnp.zeros(gate.shape, F32)
    for m in range(n_valid):
        col = gate[:, m:m + 1]
        beats = (col > gate) | ((col == gate) & (m < lane))
        rank = rank + jnp.where(beats, 1.0, 0.0)
    return rank < n_top


def _moba_prompt_body(q_ref, k_ref, v_ref, o_ref, *, tq, blk, n_blocks, n_top, scale):
    q0 = pl.program_id(2) * tq
    own = q0 // blk
    q = q_ref[...]
    qb = q.astype(BF16)
    w_lanes = _round_up(n_blocks, SUBLANES)
    lane = lax.broadcasted_iota(jnp.int32, (tq, w_lanes), 1)
    gate = jnp.full((tq, w_lanes), -jnp.inf, F32)
    for n in range(n_blocks):
        k_mean = jnp.mean(k_ref[n * blk:(n + 1) * blk, :], axis=0, keepdims=True)
        g_n = jnp.sum(q * k_mean, axis=1, keepdims=True)
        gate = jnp.where(lane == n, g_n, gate)
    gate = jnp.where(lane < own, gate, -jnp.inf)
    sel = _topk_lane_mask(gate, n_blocks, n_top) & (lane < own)
    sel_bias = jnp.where(sel, 0.0, NEG)

    def step(n, carry):
        k0 = pl.multiple_of(n * blk, blk)
        s = _dot_nt(qb, k_ref[pl.ds(k0, blk), :].astype(BF16)) * scale
        col = jnp.sum(jnp.where(lane == n, sel_bias, 0.0), axis=1, keepdims=True)
        not_own = jnp.where(n == own, 0.0, 1.0)
        s = _causal(s + col * not_own, q0, k0)
        return _softmax_step(carry, s, v_ref[pl.ds(k0, blk), :].astype(BF16))

    _, l, acc = lax.fori_loop(0, own + 1, step, _softmax_init(tq, HEAD_DIM))
    o_ref[...] = (acc / l).astype(o_ref.dtype)


def _moba_prompt(q, k, v, batch, seq, n_heads, n_kv):
    blk = MOBA_BLOCK
    tq = min(blk, 256)
    nq = seq // tq
    n_blocks = seq // blk
    group = n_heads // n_kv
    return pl.pallas_call(
        functools.partial(_moba_prompt_body, tq=tq, blk=blk, n_blocks=n_blocks,
                          n_top=min(MOBA_TOPK, n_blocks), scale=HEAD_DIM ** -0.5),
        grid=(batch, n_heads, nq),
        in_specs=[pl.BlockSpec((tq, HEAD_DIM), lambda b, h, i: (b * nq + i, h)),
                  pl.BlockSpec((seq, HEAD_DIM), lambda b, h, i: (b, h // group)),
                  pl.BlockSpec((seq, HEAD_DIM), lambda b, h, i: (b, h // group))],
        out_specs=pl.BlockSpec((tq, HEAD_DIM), lambda b, h, i: (b * nq + i, h)),
        out_shape=jax.ShapeDtypeStruct(q.shape, BF16),
        compiler_params=_cparams(3),
        name="moba_prompt",
    )(q, k, v)


def _sort_key(x):
    b = lax.bitcast_convert_type(x, jnp.int32)
    return jnp.where(b < 0, b ^ jnp.int32(0x7FFFFFFF), b)


def _kth_key(count_ge, rows, k):
    kf = jnp.float32(k)
    int_min = jnp.int32(-2 ** 31)
    t0 = jnp.where(count_ge(jnp.zeros((rows, 1), jnp.int32)) >= kf, jnp.int32(0), int_min)

    def step(it, t):
        cand = t | lax.shift_left(jnp.int32(1), jnp.int32(30) - it)
        return jnp.where(count_ge(cand) >= kf, cand, t)

    return lax.fori_loop(0, 31, step, jnp.broadcast_to(t0, (rows, 1)))


def _dsa_select_prompt_body(qi_ref, wi_ref, ki_ref, bias_ref, key_scr, *, tq, tk, n_chunks, n_idx_heads, n_keep):
    q0 = pl.program_id(1) * tq
    wi = wi_ref[...]
    for c in range(n_chunks):
        @pl.when(c * tk <= q0 + tq - 1)
        def _():
            kc = ki_ref[c * tk:(c + 1) * tk, :].astype(BF16)
            sc = jnp.zeros((tq, tk), F32)
            for h in range(n_idx_heads):
                logits = _dot_nt(qi_ref[:, h * IDX_DIM:(h + 1) * IDX_DIM], kc)
                sc = sc + wi[:, h:h + 1] * jnp.maximum(logits, 0.0)
            qpos = q0 + lax.broadcasted_iota(jnp.int32, (tq, tk), 0)
            kpos = c * tk + lax.broadcasted_iota(jnp.int32, (tq, tk), 1)
            key_scr[c] = _sort_key(jnp.where(kpos <= qpos, sc, -jnp.inf))

    n_live = (q0 + tq - 1) // tk + 1

    def count_ge(t):
        def add(c, a):
            return a + jnp.sum(jnp.where(key_scr[c] >= t, 1.0, 0.0), axis=1, keepdims=True)
        return lax.fori_loop(0, n_live, add, jnp.zeros((tq, 1), F32))

    thr = _kth_key(count_ge, tq, n_keep)
    for c in range(n_chunks):
        @pl.when(c * tk <= q0 + tq - 1)
        def _():
            qpos = q0 + lax.broadcasted_iota(jnp.int32, (tq, tk), 0)
            kpos = c * tk + lax.broadcasted_iota(jnp.int32, (tq, tk), 1)
            keep = (key_scr[c] >= thr) & (kpos <= qpos)
            bias_ref[0, c] = jnp.where(keep, 0.0, NEG).astype(bias_ref.dtype)

        @pl.when(c * tk > q0 + tq - 1)
        def _():
            bias_ref[0, c] = jnp.full((tq, tk), NEG, bias_ref.dtype)


def _dsa_select_prompt(qi, wi, ki, batch, seq, n_keep):
    tq = tk = _tile(seq, (256, 128))
    nq = seq // tq
    n_chunks = seq // tk
    n_idx_heads = wi.shape[1]
    return pl.pallas_call(
        functools.partial(_dsa_select_prompt_body, tq=tq, tk=tk, n_chunks=n_chunks,
                          n_idx_heads=n_idx_heads, n_keep=n_keep),
        grid=(batch, nq),
        in_specs=[pl.BlockSpec((tq, qi.shape[1]), lambda b, i: (b * nq + i, 0)),
                  pl.BlockSpec((tq, n_idx_heads), lambda b, i: (b * nq + i, 0)),
                  pl.BlockSpec((seq, IDX_DIM), lambda b, i: (b, 0))],
        out_specs=pl.BlockSpec((1, n_chunks, tq, tk), lambda b, i: (b, 0, i, 0)),
        out_shape=jax.ShapeDtypeStruct((batch, n_chunks, seq, tk), BF16),
        scratch_shapes=[pltpu.VMEM((n_chunks, tq, tk), jnp.int32)],
        compiler_params=_cparams(2),
        name="dsa_select_prompt",
    )(qi, wi, ki)


def _dsa_prompt_body(q_ref, k_ref, v_ref, bias_ref, o_ref, *, tq, tk, scale):
    q0 = pl.program_id(1) * tq
    q = q_ref[...]

    def step(j, carry):
        k0 = pl.multiple_of(j * tk, tk)
        s = _dot_nt(q, k_ref[pl.ds(k0, tk), :].astype(BF16)) * scale + bias_ref[0, j].astype(F32)
        return _softmax_step(carry, s, v_ref[pl.ds(k0, tk), :].astype(BF16))

    _, l, acc = lax.fori_loop(0, (q0 + tq - 1) // tk + 1, step, _softmax_init(tq, HEAD_DIM))
    o_ref[...] = (acc / l).astype(o_ref.dtype)


def _dsa_prompt(q, k, v, bias, batch, seq, n_heads):
    n_chunks, tk = bias.shape[1], bias.shape[3]
    tq = tk
    nq = seq // tq
    return pl.pallas_call(
        functools.partial(_dsa_prompt_body, tq=tq, tk=tk, scale=HEAD_DIM ** -0.5),
        grid=(batch, nq, n_heads),
        in_specs=[pl.BlockSpec((tq, HEAD_DIM), lambda b, i, h: (b * nq + i, h)),
                  pl.BlockSpec((seq, HEAD_DIM), lambda b, i, h: (b, 0)),
                  pl.BlockSpec((seq, HEAD_DIM), lambda b, i, h: (b, 0)),
                  pl.BlockSpec((1, n_chunks, tq, tk), lambda b, i, h: (b, 0, i, 0))],
        out_specs=pl.BlockSpec((tq, HEAD_DIM), lambda b, i, h: (b * nq + i, h)),
        out_shape=jax.ShapeDtypeStruct(q.shape, BF16),
        compiler_params=_cparams(3),
        name="dsa_prompt",
    )(q, k, v, bias)


def _page_specs(block_tail, layer, n_per_step):
    zeros = (0,) * len(block_tail)

    def spec(r):
        return pl.BlockSpec((1, 1) + block_tail, lambda s, g, pt: (layer, pt[s, g * n_per_step + r]) + zeros)

    return [spec(r) for r in range(n_per_step)]


def _mla_decode_body(pt_ref, ql_ref, qr_ref, cn_ref, krn_ref, *rest, n_pages_step, scale):
    lat_refs = rest[:n_pages_step]
    kr_refs = rest[n_pages_step:2 * n_pages_step]
    o_ref, m_scr, l_scr, acc_scr = rest[2 * n_pages_step:]
    g = pl.program_id(1)
    ql = ql_ref[0]
    qr = qr_ref[0]

    @pl.when(g == 0)
    def _():
        s_self = (jnp.sum(ql * cn_ref[0], axis=1, keepdims=True)
                  + jnp.sum(qr * krn_ref[0], axis=1, keepdims=True)) * scale
        m_scr[...] = s_self
        l_scr[...] = jnp.ones_like(s_self)
        acc_scr[...] = jnp.broadcast_to(cn_ref[0], acc_scr.shape)

    qlb, qrb = ql.astype(BF16), qr.astype(BF16)
    lats = [r[0, 0].astype(BF16) for r in lat_refs]
    s = jnp.concatenate([_dot_nt(qlb, lat) + _dot(qrb, kr[0, 0].astype(BF16))
                         for lat, kr in zip(lats, kr_refs)], axis=1) * scale
    m = m_scr[...]
    m_new = jnp.maximum(m, jnp.max(s, axis=1, keepdims=True))
    a = jnp.exp(m - m_new)
    p = jnp.exp(s - m_new)
    page = lats[0].shape[0]
    pv = sum(_dot(p[:, r * page:(r + 1) * page].astype(BF16), lat) for r, lat in enumerate(lats))
    m_scr[...] = m_new
    l_scr[...] = a * l_scr[...] + jnp.sum(p, axis=1, keepdims=True)
    acc_scr[...] = a * acc_scr[...] + pv

    @pl.when(g == pl.num_programs(1) - 1)
    def _():
        o_ref[0] = acc_scr[...] / l_scr[...]


def _mla_decode(q_lat, q_rope, c_new, kr_new, cache_lat, cache_kr_t, page_table, layer, scale):
    n_seq, n_heads, c_dim = q_lat.shape
    r_dim = q_rope.shape[2]
    page = cache_lat.shape[2]
    n_pages = page_table.shape[1]
    gp = _tile(n_pages, (PAGES_PER_STEP, 4, 2, 1))
    per_seq = lambda shape: pl.BlockSpec((1,) + shape, lambda s, g, pt: (s, 0, 0))
    grid_spec = pltpu.PrefetchScalarGridSpec(
        num_scalar_prefetch=1,
        grid=(n_seq, n_pages // gp),
        in_specs=[per_seq((n_heads, c_dim)), per_seq((n_heads, r_dim)), per_seq((1, c_dim)), per_seq((1, r_dim))]
        + _page_specs((page, c_dim), layer, gp) + _page_specs((r_dim, page), layer, gp),
        out_specs=per_seq((n_heads, c_dim)),
        scratch_shapes=[pltpu.VMEM((n_heads, 1), F32), pltpu.VMEM((n_heads, 1), F32),
                        pltpu.VMEM((n_heads, c_dim), F32)],
    )
    return pl.pallas_call(
        functools.partial(_mla_decode_body, n_pages_step=gp, scale=scale),
        grid_spec=grid_spec,
        out_shape=jax.ShapeDtypeStruct(q_lat.shape, F32),
        compiler_params=_cparams(2),
        name="mla_decode",
    )(page_table, q_lat, q_rope, c_new, kr_new, *([cache_lat] * gp), *([cache_kr_t] * gp))


def _moba_decode_body(pt_ref, q_ref, kn_ref, vn_ref, *rest, n_pages_step, pages_per_block, n_kv, n_top, scale):
    k_refs = rest[:n_pages_step]
    v_refs = rest[n_pages_step:2 * n_pages_step]
    o_ref, gate_scr, m_scr, l_scr, oblk_scr = rest[2 * n_pages_step:]
    g = pl.program_id(1)
    n_groups = pl.num_programs(1)
    q = q_ref[0]
    qb = q.astype(BF16)
    n_heads = q.shape[0]
    group = n_heads // n_kv
    head_kv = lax.broadcasted_iota(jnp.int32, (n_heads, 1), 0) // group
    lane = lax.broadcasted_iota(jnp.int32, gate_scr.shape, 1)
    blocks_step = n_pages_step // pages_per_block

    @pl.when(g == 0)
    def _():
        for scr in (gate_scr, m_scr, l_scr):
            scr[...] = jnp.zeros(scr.shape, F32)

    def per_kv(vals):
        out = vals[0]
        for kh in range(1, n_kv):
            out = jnp.where(head_kv == kh, vals[kh], out)
        return out

    for jb in range(blocks_step):
        n = g * blocks_step + jb
        refs = range(jb * pages_per_block, (jb + 1) * pages_per_block)
        gates, scores, vs = [], [], []
        for kh in range(n_kv):
            k_blk = jnp.concatenate([k_refs[r][0, 0, :, kh, :] for r in refs], axis=0)
            vs.append(jnp.concatenate([v_refs[r][0, 0, :, kh, :] for r in refs], axis=0).astype(BF16))
            k_mean = jnp.mean(k_blk, axis=0, keepdims=True)
            gates.append(jnp.sum(q * k_mean, axis=1, keepdims=True))
            scores.append(_dot_nt(qb, k_blk.astype(BF16)))
        s = per_kv(scores) * scale
        m_n = jnp.max(s, axis=1, keepdims=True)
        p = jnp.exp(s - m_n)
        pb = p.astype(BF16)
        gate_scr[...] = jnp.where(lane == n, per_kv(gates), gate_scr[...])
        m_scr[...] = jnp.where(lane == n, m_n, m_scr[...])
        l_scr[...] = jnp.where(lane == n, jnp.sum(p, axis=1, keepdims=True), l_scr[...])
        oblk_scr[n] = per_kv([_dot(pb, v) for v in vs])

    @pl.when(g == n_groups - 1)
    def _():
        n_past = n_groups * blocks_step
        in_past = lane < n_past
        gate = jnp.where(in_past, gate_scr[...], -jnp.inf)
        sel = _topk_lane_mask(gate, n_past, n_top) & in_past
        k_self = per_kv([kn_ref[0, :, kh, :] for kh in range(n_kv)])
        v_self = per_kv([vn_ref[0, :, kh, :] for kh in range(n_kv)])
        s_self = jnp.sum(q * k_self, axis=1, keepdims=True) * scale
        m_blk = jnp.where(sel, m_scr[...], NEG)
        m_tot = jnp.maximum(jnp.max(m_blk, axis=1, keepdims=True), s_self)
        w_blk = jnp.where(sel, jnp.exp(m_blk - m_tot), 0.0)
        w_self = jnp.exp(s_self - m_tot)
        l_tot = jnp.sum(w_blk * l_scr[...], axis=1, keepdims=True) + w_self
        o = w_self * v_self
        for nb in range(n_past):
            o = o + w_blk[:, nb:nb + 1] * oblk_scr[nb]
        o_ref[0] = o / l_tot


def _moba_decode(q, k_new, v_new, cache_k, cache_v, page_table, layer):
    n_seq, n_heads, _ = q.shape
    page, n_kv = cache_k.shape[2], cache_k.shape[3]
    n_pages = page_table.shape[1]
    ppb = MOBA_BLOCK // page
    gp = _tile(n_pages, (PAGES_PER_STEP, 4, 2, 1))
    assert MOBA_BLOCK % page == 0 and gp % ppb == 0 and (n_pages * page) % MOBA_BLOCK == 0
    n_past = n_pages // ppb
    assert n_past <= LANES
    per_seq = lambda shape: pl.BlockSpec((1,) + shape, lambda s, g, pt: (s,) + (0,) * len(shape))
    grid_spec = pltpu.PrefetchScalarGridSpec(
        num_scalar_prefetch=1,
        grid=(n_seq, n_pages // gp),
        in_specs=[per_seq((n_heads, HEAD_DIM)), per_seq((1, n_kv, HEAD_DIM)), per_seq((1, n_kv, HEAD_DIM))]
        + _page_specs((page, n_kv, HEAD_DIM), layer, gp) + _page_specs((page, n_kv, HEAD_DIM), layer, gp),
        out_specs=per_seq((n_heads, HEAD_DIM)),
        scratch_shapes=[pltpu.VMEM((n_heads, LANES), F32)] * 3 + [pltpu.VMEM((n_past, n_heads, HEAD_DIM), F32)],
    )
    return pl.pallas_call(
        functools.partial(_moba_decode_body, n_pages_step=gp, pages_per_block=ppb, n_kv=n_kv,
                          n_top=min(MOBA_TOPK, n_past + 1), scale=HEAD_DIM ** -0.5),
        grid_spec=grid_spec,
        out_shape=jax.ShapeDtypeStruct(q.shape, F32),
        compiler_params=_cparams(2),
        name="moba_decode",
    )(page_table, q, k_new, v_new, *([cache_k] * gp), *([cache_v] * gp))


def _dsa_score_decode_body(pt_ref, qi_ref, wi_ref, kin_ref, *rest, n_pages_step):
    ki_refs = rest[:n_pages_step]
    o_ref = rest[n_pages_step]
    g = pl.program_id(1)
    qi = qi_ref[0]
    wi = wi_ref[0]

    @pl.when(g < pl.num_programs(1) - 1)
    def _():
        qb = qi.astype(BF16)
        logits = jnp.concatenate([_dot(qb, r[0, 0].astype(BF16)) for r in ki_refs], axis=1)
        o_ref[0] = jnp.sum(wi * jnp.maximum(logits, 0.0), axis=0, keepdims=True)

    @pl.when(g == pl.num_programs(1) - 1)
    def _():
        logit = jnp.sum(qi * kin_ref[0], axis=1, keepdims=True)
        sc = jnp.sum(wi * jnp.maximum(logit, 0.0), axis=0, keepdims=True)
        lane = lax.broadcasted_iota(jnp.int32, o_ref.shape[1:], 1)
        o_ref[0] = jnp.where(lane == 0, sc, -jnp.inf)


def _dsa_score_decode(qi, wi, ki_new, cache_ki_t, page_table, layer):
    n_seq, n_ih, _ = qi.shape
    page = cache_ki_t.shape[3]
    n_pages = page_table.shape[1]
    gp = _tile(n_pages, (PAGES_PER_STEP, 4, 2, 1))
    n_groups = n_pages // gp
    per_seq = lambda shape: pl.BlockSpec((1,) + shape, lambda s, g, pt: (s, 0, 0))
    zeros = (0, 0)

    def page_spec(r):
        return pl.BlockSpec((1, 1, IDX_DIM, page),
                            lambda s, g, pt: (layer, pt[s, jnp.minimum(g, n_groups - 1) * gp + r]) + zeros)

    grid_spec = pltpu.PrefetchScalarGridSpec(
        num_scalar_prefetch=1,
        grid=(n_seq, n_groups + 1),
        in_specs=[per_seq((n_ih, IDX_DIM)), per_seq((n_ih, 1)), per_seq((1, IDX_DIM))]
        + [page_spec(r) for r in range(gp)],
        out_specs=pl.BlockSpec((1, 1, gp * page), lambda s, g, pt: (s, 0, g)),
    )
    return pl.pallas_call(
        functools.partial(_dsa_score_decode_body, n_pages_step=gp),
        grid_spec=grid_spec,
        out_shape=jax.ShapeDtypeStruct((n_seq, 1, (n_groups + 1) * gp * page), F32),
        compiler_params=_cparams(2),
        name="dsa_score_decode",
    )(page_table, qi, wi, ki_new, *([cache_ki_t] * gp))


def _dsa_select_decode_body(sc_ref, bias_ref, *, n_keys, n_keep):
    key = _sort_key(sc_ref[...])
    rows = key.shape[0]
    thr = _kth_key(lambda t: jnp.sum(jnp.where(key >= t, 1.0, 0.0), axis=1, keepdims=True), rows, n_keep)
    lane = lax.broadcasted_iota(jnp.int32, key.shape, 1)
    bias_ref[...] = jnp.where((key >= thr) & (lane < n_keys), 0.0, NEG)


def _dsa_select_decode(scores, n_keys, n_keep):
    return pl.pallas_call(
        functools.partial(_dsa_select_decode_body, n_keys=n_keys, n_keep=n_keep),
        out_shape=jax.ShapeDtypeStruct(scores.shape, F32),
        compiler_params=pltpu.CompilerParams(vmem_limit_bytes=VMEM_LIMIT),
        name="dsa_select_decode",
    )(scores)


def _dsa_decode_body(pt_ref, q_ref, kn_ref, vn_ref, bias_ref, bself_ref, *rest, n_pages_step, scale):
    k_refs = rest[:n_pages_step]
    v_refs = rest[n_pages_step:2 * n_pages_step]
    o_ref, m_scr, l_scr, acc_scr = rest[2 * n_pages_step:]
    g = pl.program_id(1)
    q = q_ref[0]

    @pl.when(g == 0)
    def _():
        s_self = jnp.sum(q * kn_ref[0], axis=1, keepdims=True) * scale + bself_ref[0][:, 0:1]
        m_scr[...] = jnp.maximum(s_self, M_INIT)
        p_self = jnp.exp(s_self - m_scr[...])
        l_scr[...] = p_self
        acc_scr[...] = p_self * vn_ref[0]

    qb = q.astype(BF16)
    s = jnp.concatenate([_dot_nt(qb, r[0, 0].astype(BF16)) for r in k_refs], axis=1) * scale + bias_ref[0]
    m = m_scr[...]
    m_new = jnp.maximum(m, jnp.max(s, axis=1, keepdims=True))
    a = jnp.exp(m - m_new)
    p = jnp.exp(s - m_new)
    page = k_refs[0].shape[2]
    pv = sum(_dot(p[:, r * page:(r + 1) * page].astype(BF16), v[0, 0].astype(BF16)) for r, v in enumerate(v_refs))
    m_scr[...] = m_new
    l_scr[...] = a * l_scr[...] + jnp.sum(p, axis=1, keepdims=True)
    acc_scr[...] = a * acc_scr[...] + pv

    @pl.when(g == pl.num_programs(1) - 1)
    def _():
        o_ref[0] = acc_scr[...] / l_scr[...]


def _dsa_decode(q, k_new, v_new, bias, cache_k, cache_v, page_table, layer):
    n_seq, n_heads, _ = q.shape
    page = cache_k.shape[2]
    n_pages = page_table.shape[1]
    gp = _tile(n_pages, (PAGES_PER_STEP, 4, 2, 1))
    n_groups = n_pages // gp
    per_seq = lambda shape: pl.BlockSpec((1,) + shape, lambda s, g, pt: (s, 0, 0))
    grid_spec = pltpu.PrefetchScalarGridSpec(
        num_scalar_prefetch=1,
        grid=(n_seq, n_groups),
        in_specs=[per_seq((n_heads, HEAD_DIM)), per_seq((1, HEAD_DIM)), per_seq((1, HEAD_DIM)),
                  pl.BlockSpec((1, 1, gp * page), lambda s, g, pt: (s, 0, g)),
                  pl.BlockSpec((1, 1, gp * page), lambda s, g, pt: (s, 0, n_groups))]
        + _page_specs((page, HEAD_DIM), layer, gp) + _page_specs((page, HEAD_DIM), layer, gp),
        out_specs=per_seq((n_heads, HEAD_DIM)),
        scratch_shapes=[pltpu.VMEM((n_heads, 1), F32), pltpu.VMEM((n_heads, 1), F32),
                        pltpu.VMEM((n_heads, HEAD_DIM), F32)],
    )
    return pl.pallas_call(
        functools.partial(_dsa_decode_body, n_pages_step=gp, scale=HEAD_DIM ** -0.5),
        grid_spec=grid_spec,
        out_shape=jax.ShapeDtypeStruct(q.shape, F32),
        compiler_params=_cparams(2),
        name="dsa_decode",
    )(page_table, q, k_new, v_new, bias, bias, *([cache_k] * gp), *([cache_v] * gp))


def _silu_gate(u_conv, g):
    return u_conv / (1.0 + jnp.exp(-u_conv)) * g


def _ffn_up_prompt_body(x_ref, wu_ref, wg_ref, cw_ref, cb_ref, act_ref, tail_ref, u_scr, *, tm, tiles_per_seq):
    i = pl.program_id(1)

    @pl.when(i % tiles_per_seq == 0)
    def _():
        u_scr[0:SUBLANES, :] = jnp.zeros((SUBLANES, u_scr.shape[1]), F32)

    x = x_ref[...]
    u = _dot(x, wu_ref[...])
    g = _dot(x, wg_ref[...])
    u_scr[SUBLANES:SUBLANES + tm, :] = u
    u_conv = (cb_ref[...] + cw_ref[0:1, :] * u_scr[SUBLANES - 2:SUBLANES - 2 + tm, :]
              + cw_ref[1:2, :] * u_scr[SUBLANES - 1:SUBLANES - 1 + tm, :] + cw_ref[2:3, :] * u)
    act_ref[...] = _silu_gate(u_conv, g).astype(act_ref.dtype)
    tail = u_scr[tm:tm + SUBLANES, :]
    u_scr[0:SUBLANES, :] = tail
    tail_ref[0] = tail


def _ffn_up_prompt(x, w_in, conv_w, conv_b, batch, seq):
    m, d = x.shape
    f = w_in.shape[1] // 2
    tm = _tile(seq, (1024, 512, 256, 128))
    tf = _tile(f, (512, 256, 128))
    nf = f // tf
    tps = seq // tm
    return pl.pallas_call(
        functools.partial(_ffn_up_prompt_body, tm=tm, tiles_per_seq=tps),
        grid=(nf, m // tm),
        in_specs=[pl.BlockSpec((tm, d), lambda j, i: (i, 0)),
                  pl.BlockSpec((d, tf), lambda j, i: (0, j)),
                  pl.BlockSpec((d, tf), lambda j, i: (0, j + nf)),
                  pl.BlockSpec((conv_w.shape[0], tf), lambda j, i: (0, j)),
                  pl.BlockSpec((1, tf), lambda j, i: (0, j))],
        out_specs=[pl.BlockSpec((tm, tf), lambda j, i: (i, j)),
                   pl.BlockSpec((1, SUBLANES, tf), lambda j, i: (i // tps, 0, j))],
        out_shape=[jax.ShapeDtypeStruct((m, f), BF16), jax.ShapeDtypeStruct((batch, SUBLANES, f), F32)],
        scratch_shapes=[pltpu.VMEM((tm + SUBLANES, tf), F32)],
        compiler_params=_cparams(2),
        name="ffn_up_prompt",
    )(x, w_in, w_in, conv_w, conv_b.reshape(1, f))


def _ffn_up_decode_body(x_ref, wu_ref, wg_ref, cw_ref, cb_ref, p2_ref, p1_ref, act_ref, u_ref):
    x = x_ref[...]
    u = _dot(x, wu_ref[...])
    g = _dot(x, wg_ref[...])
    u_conv = cb_ref[...] + cw_ref[0:1, :] * p2_ref[...] + cw_ref[1:2, :] * p1_ref[...] + cw_ref[2:3, :] * u
    act_ref[...] = _silu_gate(u_conv, g).astype(act_ref.dtype)
    u_ref[...] = u


def _ffn_up_decode(x, w_in, conv_w, conv_b, prev2, prev1):
    m, d = x.shape
    f = w_in.shape[1] // 2
    tf = _tile(f, (512, 256, 128))
    nf = f // tf
    col = lambda rows: pl.BlockSpec((rows, tf), lambda j: (0, j))
    return pl.pallas_call(
        _ffn_up_decode_body,
        grid=(nf,),
        in_specs=[pl.BlockSpec((m, d), lambda j: (0, 0)), pl.BlockSpec((d, tf), lambda j: (0, j)),
                  pl.BlockSpec((d, tf), lambda j: (0, j + nf)), col(conv_w.shape[0]), col(1), col(m), col(m)],
        out_specs=[col(m), col(m)],
        out_shape=[jax.ShapeDtypeStruct((m, f), BF16), jax.ShapeDtypeStruct((m, f), F32)],
        compiler_params=_cparams(1),
        name="ffn_up_decode",
    )(x, w_in, w_in, conv_w, conv_b.reshape(1, f), prev2, prev1)


def _pad_cols(w, n):
    return jnp.pad(w, ((0, 0), (0, n - w.shape[1])))


def _even_weights(w_in, w_q_up, w_uk, w_uv, w_out, q_lora, kv_lora, rope_dim, n_moba, n_kv):
    sizes = (q_lora, kv_lora, rope_dim, n_moba * HEAD_DIM, n_kv * HEAD_DIM, n_kv * HEAD_DIM)
    cuts = [0]
    for s in sizes:
        cuts.append(cuts[-1] + s)
    cq, ckv, kr, mq, mk, mv = [w_in[:, a:b] for a, b in zip(cuts[:-1], cuts[1:])]
    order = (cq, ckv, mq, mk, mv, kr)
    offs = [0]
    for p in order:
        offs.append(offs[-1] + p.shape[1])
    n_pad = _round_up(offs[-2] + LANES, 2 * LANES)
    w_in_r = _pad_cols(jnp.concatenate(order, axis=1), n_pad).astype(BF16)
    n_heads = w_uk.shape[1]
    nope = w_uk.shape[2]
    wq = w_q_up.reshape(q_lora, n_heads, nope + rope_dim)
    wq_r = jnp.concatenate([wq[:, :, :nope].reshape(q_lora, -1), wq[:, :, nope:].reshape(q_lora, -1)], axis=1).astype(BF16)
    return dict(w_in=w_in_r, offs=tuple(offs), w_q=wq_r,
                w_uk_flat=w_uk.reshape(kv_lora, -1).astype(BF16), w_uv_flat=w_uv.reshape(kv_lora, -1).astype(BF16),
                w_uk_t=jnp.transpose(w_uk, (1, 2, 0)).astype(BF16), w_uv_h=jnp.transpose(w_uv, (1, 0, 2)).astype(BF16),
                w_out=w_out.astype(BF16))


def _even_project(hn, ew, q_norm, kv_norm, tabs_head, tabs_mla, rope_dim, n_mla):
    z = _mm(hn, ew["w_in"])
    cqn, ckv, kr, mq, mk, mv = _post_even(z, q_norm, kv_norm, tabs_head, tabs_mla, ew["offs"], rope_dim)
    qn, qr = _post_q(_mm(cqn, ew["w_q"]), tabs_mla, n_mla, rope_dim)
    return qn, qr, ckv, kr, mq, mk, mv


def _ffn(h, hn, w_in, conv_w, conv_b, w_out, prompt_dims=None, state=None):
    if prompt_dims is not None:
        act, tail = _ffn_up_prompt(hn, w_in, conv_w, conv_b, *prompt_dims)
        new_state = tail[:, SUBLANES - (conv_w.shape[0] - 1):, :]
    else:
        act, u = _ffn_up_decode(hn, w_in, conv_w, conv_b, state[:, 0, :], state[:, 1, :])
        new_state = jnp.stack([state[:, 1, :], u], axis=1)
    return _mm(act, w_out, res=h), new_state


def kernel(x_prompt, x_sample, cache_mla_latent, cache_mla_krope, cache_moba_k, cache_moba_v, cache_dsa_k, cache_dsa_v, cache_dsa_kidx, state_ffn_conv, page_table, norm_attn, norm_ffn, norm_final, w_in_ab, mla_q_norm, mla_w_q_up, mla_kv_norm, mla_w_uk, mla_w_uv, w_out_ab, w_in_c, w_out_c, w_ffn_in, conv_w, conv_b, w_ffn_out):
    batch, seq, d_model = x_prompt.shape
    n_dec, dec_seq, _ = x_sample.shape
    depth = norm_attn.shape[0]
    assert dec_seq == 1 and conv_w.shape[1] == 3 and seq & (seq - 1) == 0
    page = cache_mla_latent.shape[2]
    past_len = page_table.shape[1] * page
    q_lora, kv_lora, rope_dim = mla_q_norm.shape[1], mla_kv_norm.shape[1], cache_mla_krope.shape[3]
    n_mla, nope = mla_w_uk.shape[2], mla_w_uk.shape[3]
    n_kv = cache_moba_k.shape[3]
    n_moba = (w_in_ab.shape[2] - q_lora - kv_lora - rope_dim) // HEAD_DIM - 2 * n_kv
    n_dsa = w_out_c.shape[1] // HEAD_DIM
    n_idx = (w_in_c.shape[2] - (n_dsa + 2) * HEAD_DIM - IDX_DIM) // (IDX_DIM + 1)
    assert nope == HEAD_DIM and mla_w_uv.shape[3] == HEAD_DIM and cache_dsa_kidx.shape[3] == IDX_DIM
    tp = batch * seq
    mla_scale = 1.0 / math.sqrt(nope + rope_dim)

    pos_p = jnp.tile(jnp.arange(seq, dtype=jnp.int32), batch)
    pos_s = jnp.full((n_dec,), past_len, jnp.int32)
    tabs = {}
    for name, pos in (("p", pos_p), ("s", pos_s)):
        tabs[name] = dict(head=_rope_tables(pos, ROT_DIM, HEAD_DIM), mla=_rope_tables(pos, rope_dim, rope_dim),
                          idx=_rope_tables(pos, IDX_DIM // 4, IDX_DIM))

    cache_kr_t = jnp.swapaxes(cache_mla_krope, 2, 3)
    cache_ki_t = jnp.swapaxes(cache_dsa_kidx, 2, 3)

    h_p = x_prompt.reshape(tp, d_model)
    h_s = x_sample.reshape(n_dec, d_model)
    outs = {k: [] for k in ("lat_p", "lat_s", "kr_p", "kr_s", "mk_p", "mk_s", "mv_p", "mv_s",
                            "dk_p", "dk_s", "dv_p", "dv_s", "dki_p", "dki_s", "cv_p", "cv_s")}

    for layer in range(depth):
        i = layer // 2
        hn_p = _rmsnorm(h_p, norm_attn[layer], BF16)
        hn_s = _rmsnorm(h_s, norm_attn[layer], BF16)
        if layer % 2 == 0:
            ew = _even_weights(w_in_ab[i], mla_w_q_up[i], mla_w_uk[i], mla_w_uv[i], w_out_ab[i],
                               q_lora, kv_lora, rope_dim, n_moba, n_kv)
            qn, qr, ckv, kr, mq, mk, mv = _even_project(hn_p, ew, mla_q_norm[i], mla_kv_norm[i],
                                                        tabs["p"]["head"], tabs["p"]["mla"], rope_dim, n_mla)
            ckv_b = ckv.astype(BF16)
            o_mla = _mla_prompt(qn, qr, _mm(ckv_b, ew["w_uk_flat"], out_dtype=BF16), kr,
                                _mm(ckv_b, ew["w_uv_flat"], out_dtype=BF16), batch, seq, n_mla, mla_scale)
            o_moba = _moba_prompt(mq, mk, mv, batch, seq, n_moba, n_kv)
            h_p = _mm(jnp.concatenate([o_mla, o_moba], axis=1), ew["w_out"], res=h_p)
            outs["lat_p"].append(ckv.reshape(batch, seq, kv_lora))
            outs["kr_p"].append(kr.reshape(batch, seq, rope_dim))
            outs["mk_p"].append(mk.reshape(batch, seq, n_kv, HEAD_DIM))
            outs["mv_p"].append(mv.reshape(batch, seq, n_kv, HEAD_DIM))
            qn, qr, ckv, kr, mq, mk, mv = _even_project(hn_s, ew, mla_q_norm[i], mla_kv_norm[i],
                                                        tabs["s"]["head"], tabs["s"]["mla"], rope_dim, n_mla)
            q_lat = _bmm_heads(qn, ew["w_uk_t"]).reshape(n_dec, n_mla, kv_lora)
            o_lat = _mla_decode(q_lat, jnp.transpose(qr, (1, 0, 2)).astype(F32), ckv.reshape(n_dec, 1, kv_lora),
                                kr.reshape(n_dec, 1, rope_dim), cache_mla_latent, cache_kr_t,
                                page_table, i, mla_scale)
            o_mla = _bmm_heads(o_lat.reshape(n_dec, n_mla * kv_lora), ew["w_uv_h"])
            o_moba = _moba_decode(mq.reshape(n_dec, n_moba, HEAD_DIM), mk.reshape(n_dec, 1, n_kv, HEAD_DIM),
                                  mv.reshape(n_dec, 1, n_kv, HEAD_DIM), cache_moba_k, cache_moba_v, page_table, i)
            o_cat = jnp.concatenate([o_mla, o_moba.reshape(n_dec, n_moba * HEAD_DIM)], axis=1).astype(BF16)
            h_s = _mm(o_cat, ew["w_out"], res=h_s)
            outs["lat_s"].append(ckv.reshape(n_dec, 1, kv_lora))
            outs["kr_s"].append(kr.reshape(n_dec, 1, rope_dim))
            outs["mk_s"].append(mk.reshape(n_dec, 1, n_kv, HEAD_DIM))
            outs["mv_s"].append(mv.reshape(n_dec, 1, n_kv, HEAD_DIM))
        else:
            e_in = w_in_c.shape[2]
            w_in = _pad_cols(w_in_c[i], _round_up(e_in - IDX_DIM - n_idx + LANES, 2 * LANES)).astype(BF16)
            w_out = w_out_c[i].astype(BF16)
            o_k = n_dsa * HEAD_DIM
            offs = (0, o_k, o_k + HEAD_DIM, o_k + 2 * HEAD_DIM, o_k + 2 * HEAD_DIM + n_idx * IDX_DIM)
            q, k, v, qi, ki, wi = _post_odd(_mm(hn_p, w_in), tabs["p"]["head"], tabs["p"]["idx"], offs, n_idx)
            bias = _dsa_select_prompt(qi, wi, ki, batch, seq, min(DSA_TOPK, seq // 4))
            o = _dsa_prompt(q, k, v, bias, batch, seq, n_dsa)
            h_p = _mm(o, w_out, res=h_p)
            outs["dk_p"].append(k.reshape(batch, seq, HEAD_DIM))
            outs["dv_p"].append(v.reshape(batch, seq, HEAD_DIM))
            outs["dki_p"].append(ki.reshape(batch, seq, IDX_DIM))
            q, k, v, qi, ki, wi = _post_odd(_mm(hn_s, w_in), tabs["s"]["head"], tabs["s"]["idx"], offs, n_idx)
            scores = _dsa_score_decode(qi.astype(F32).reshape(n_dec, n_idx, IDX_DIM), wi.reshape(n_dec, n_idx, 1),
                                       ki.reshape(n_dec, 1, IDX_DIM), cache_ki_t, page_table, i)
            n_keys = past_len + 1
            bias = _dsa_select_decode(scores.reshape(n_dec, -1), n_keys, min(DSA_TOPK, n_keys // 4))
            o = _dsa_decode(q.astype(F32).reshape(n_dec, n_dsa, HEAD_DIM), k.reshape(n_dec, 1, HEAD_DIM),
                            v.reshape(n_dec, 1, HEAD_DIM), bias.reshape(n_dec, 1, -1),
                            cache_dsa_k, cache_dsa_v, page_table, i)
            h_s = _mm(o.reshape(n_dec, n_dsa * HEAD_DIM).astype(BF16), w_out, res=h_s)
            outs["dk_s"].append(k.reshape(n_dec, 1, HEAD_DIM))
            outs["dv_s"].append(v.reshape(n_dec, 1, HEAD_DIM))
            outs["dki_s"].append(ki.reshape(n_dec, 1, IDX_DIM))

        w_fi = w_ffn_in[layer].astype(BF16)
        w_fo = w_ffn_out[layer].astype(BF16)
        h_p, st_p = _ffn(h_p, _rmsnorm(h_p, norm_ffn[layer], BF16), w_fi, conv_w[layer], conv_b[layer], w_fo,
                         prompt_dims=(batch, seq))
        h_s, st_s = _ffn(h_s, _rmsnorm(h_s, norm_ffn[layer], BF16), w_fi, conv_w[layer], conv_b[layer], w_fo,
                         state=state_ffn_conv[layer])
        outs["cv_p"].append(st_p)
        outs["cv_s"].append(st_s)

    y_p = _rmsnorm(h_p, norm_final, F32).reshape(batch, seq, d_model)
    y_s = _rmsnorm(h_s, norm_final, F32).reshape(n_dec, 1, d_model)
    order = ("lat_p", "lat_s", "kr_p", "kr_s", "mk_p", "mk_s", "mv_p", "mv_s",
             "dk_p", "dk_s", "dv_p", "dv_s", "dki_p", "dki_s", "cv_p", "cv_s")
    return (y_p, y_s) + tuple(jnp.stack(outs[k]) for k in order)
```

```python
import functools
import math

import jax
import jax.numpy as jnp
from jax import lax
from jax.experimental import pallas as pl
from jax.experimental.pallas import tpu as pltpu

HEAD_DIM = 128
ROT_DIM = HEAD_DIM // 4
ROPE_THETA = 500000.0
RMS_EPS = 1e-6
MOBA_BLOCK = 256
MOBA_TOPK = 3
DSA_TOPK = 256
IDX_DIM = 64
NEG = -1e30
M_INIT = -1e29
LANES = 128
SUBLANES = 8
VMEM_LIMIT = 56 * 1024 * 1024
MM_VMEM_BUDGET = 46 * 1024 * 1024
MM_K_SPLIT = 4096
PAGES_PER_STEP = 16
SCORE_PAGES_PER_STEP = 32
HEADS_PER_STEP = 4

F32 = jnp.float32
BF16 = jnp.bfloat16
NT = (((1,), (1,)), ((), ()))


def _cparams(n_axes):
    return pltpu.CompilerParams(dimension_semantics=("arbitrary",) * n_axes, vmem_limit_bytes=VMEM_LIMIT)


def _tile(n, cands):
    for c in cands:
        if n % c == 0:
            return c
    return n


def _round_up(n, m):
    return -(-n // m) * m


def _dot(a, b):
    return jnp.dot(a, b, preferred_element_type=F32)


def _dot_nt(a, b):
    return lax.dot_general(a, b, NT, preferred_element_type=F32)


def _mm_body(*refs, has_res, n_k):
    if has_res:
        x_ref, w_ref, r_ref, o_ref = refs[:4]
    else:
        x_ref, w_ref, o_ref = refs[:3]
    part = _dot(x_ref[...], w_ref[...])

    def finish(acc):
        if has_res:
            acc = acc + r_ref[...]
        o_ref[...] = acc.astype(o_ref.dtype)

    if n_k == 1:
        finish(part)
        return
    acc_ref = refs[-1]
    kk = pl.program_id(2)

    @pl.when(kk == 0)
    def _():
        acc_ref[...] = part

    @pl.when((kk > 0) & (kk < n_k - 1))
    def _():
        acc_ref[...] += part

    @pl.when(kk == n_k - 1)
    def _():
        finish(acc_ref[...] + part)


def _mm_tiles(m, k, n):
    for tm in (1024, 512, 256, 128):
        if m % tm:
            continue
        for tk in (k, MM_K_SPLIT):
            if k % tk:
                continue
            for tn in (512, 256, 128):
                if n % tn:
                    continue
                if 2 * (tm * tk * 2 + tk * tn * 2 + 2 * tm * tn * 4) + tm * tn * 4 <= MM_VMEM_BUDGET:
                    return tm, tn, tk
    return _tile(m, (128, 8)), _tile(n, (128,)), k


def _mm(x, w, res=None, out_dtype=F32):
    m, k = x.shape
    n = w.shape[1]
    tm, tn, tk = _mm_tiles(m, k, n)
    n_k = k // tk
    in_specs = [pl.BlockSpec((tm, tk), lambda i, j, kk: (i, kk)), pl.BlockSpec((tk, tn), lambda i, j, kk: (kk, j))]
    args = [x, w]
    if res is not None:
        in_specs.append(pl.BlockSpec((tm, tn), lambda i, j, kk: (i, j)))
        args.append(res)
    return pl.pallas_call(
        functools.partial(_mm_body, has_res=res is not None, n_k=n_k),
        grid=(m // tm, n // tn, n_k),
        in_specs=in_specs,
        out_specs=pl.BlockSpec((tm, tn), lambda i, j, kk: (i, j)),
        out_shape=jax.ShapeDtypeStruct((m, n), out_dtype),
        scratch_shapes=[pltpu.VMEM((tm, tn), F32)] if n_k > 1 else [],
        compiler_params=_cparams(3),
        name="mm",
    )(*args)


def _bmm_body(x_ref, w_ref, o_ref):
    o_ref[...] = _dot(x_ref[...].astype(BF16), w_ref[0]).astype(o_ref.dtype)


def _bmm_heads(x, w):
    r = x.shape[0]
    h, k, n = w.shape
    return pl.pallas_call(
        _bmm_body,
        grid=(h,),
        in_specs=[pl.BlockSpec((r, k), lambda i: (0, i)), pl.BlockSpec((1, k, n), lambda i: (i, 0, 0))],
        out_specs=pl.BlockSpec((r, n), lambda i: (0, i)),
        out_shape=jax.ShapeDtypeStruct((r, h * n), F32),
        compiler_params=_cparams(1),
        name="bmm_heads",
    )(x, w)


def _rms(x, g):
    return x * lax.rsqrt(jnp.mean(x * x, axis=-1, keepdims=True) + RMS_EPS) * g


def _rms_body(x_ref, g_ref, o_ref):
    o_ref[...] = _rms(x_ref[...], g_ref[...]).astype(o_ref.dtype)


def _rmsnorm(x, g, out_dtype):
    m, d = x.shape
    tm = _tile(m, (256, 128, 8))
    return pl.pallas_call(
        _rms_body,
        grid=(m // tm,),
        in_specs=[pl.BlockSpec((tm, d), lambda i: (i, 0)), pl.BlockSpec((1, d), lambda i: (0, 0))],
        out_specs=pl.BlockSpec((tm, d), lambda i: (i, 0)),
        out_shape=jax.ShapeDtypeStruct((m, d), out_dtype),
        compiler_params=_cparams(1),
        name="rmsnorm",
    )(x, g.reshape(1, d))


def _rope_tables(pos, rot_dim, period):
    half = rot_dim // 2
    inv = ROPE_THETA ** (-jnp.arange(half, dtype=F32) * 2.0 / rot_dim)
    ang = pos.astype(F32)[:, None] * inv[None, :]
    cos, sin = jnp.cos(ang), jnp.sin(ang)
    t = pos.shape[0]
    z_half = jnp.zeros((t, half), F32)
    z_rest = jnp.zeros((t, period - rot_dim), F32)
    c = jnp.concatenate([cos, cos, jnp.ones((t, period - rot_dim), F32)], axis=1)
    sm = jnp.concatenate([-sin, z_half, z_rest], axis=1)
    sp = jnp.concatenate([z_half, sin, z_rest], axis=1)
    reps = LANES // period
    return tuple(jnp.tile(a, (1, reps)) for a in (c, sm, sp))


def _rope128(x, tabs, half):
    c, sm, sp = tabs
    return x * c + pltpu.roll(x, LANES - half, 1) * sm + pltpu.roll(x, half, 1) * sp


def _post_even_body(z_ref, qn_ref, kvn_ref, c1, s1m, s1p, c2, s2m, s2p,
                    cqn_ref, ckv_ref, kr_ref, mq_ref, mk_ref, mv_ref, *, offs, rope_dim):
    o_cq, o_ckv, o_mq, o_mk, o_mv, o_kr, _ = offs
    cqn_ref[...] = _rms(z_ref[:, o_cq:o_ckv], qn_ref[...]).astype(cqn_ref.dtype)
    ckv_ref[...] = _rms(z_ref[:, o_ckv:o_mq], kvn_ref[...])
    t1 = (c1[...], s1m[...], s1p[...])
    for g in range((o_mk - o_mq) // LANES):
        mq_ref[:, g * LANES:(g + 1) * LANES] = _rope128(z_ref[:, o_mq + g * LANES:o_mq + (g + 1) * LANES], t1, ROT_DIM // 2)
    for g in range((o_mv - o_mk) // LANES):
        mk_ref[:, g * LANES:(g + 1) * LANES] = _rope128(z_ref[:, o_mk + g * LANES:o_mk + (g + 1) * LANES], t1, ROT_DIM // 2)
    mv_ref[...] = z_ref[:, o_mv:o_kr]
    t2 = (c2[...], s2m[...], s2p[...])
    kr_ref[...] = _rope128(z_ref[:, o_kr:o_kr + LANES], t2, rope_dim // 2)[:, :rope_dim]


def _post_even(z, q_norm, kv_norm, tabs_head, tabs_mla, offs, rope_dim):
    m, nz = z.shape
    o_cq, o_ckv, o_mq, o_mk, o_mv, o_kr, _ = offs
    tm = _tile(m, (256, 128, 8))
    row = lambda w: pl.BlockSpec((tm, w), lambda i: (i, 0))
    full = lambda w: pl.BlockSpec((1, w), lambda i: (0, 0))
    widths = (o_ckv - o_cq, o_mq - o_ckv, rope_dim, o_mk - o_mq, o_mv - o_mk, o_kr - o_mv)
    dtypes = (BF16, F32, F32, F32, F32, F32)
    return pl.pallas_call(
        functools.partial(_post_even_body, offs=offs, rope_dim=rope_dim),
        grid=(m // tm,),
        in_specs=[row(nz), full(widths[0]), full(widths[1])] + [row(LANES)] * 6,
        out_specs=[row(w) for w in widths],
        out_shape=[jax.ShapeDtypeStruct((m, w), d) for w, d in zip(widths, dtypes)],
        compiler_params=_cparams(1),
        name="post_even",
    )(z, q_norm.reshape(1, -1), kv_norm.reshape(1, -1), *tabs_head, *tabs_mla)


def _post_q_body(q_ref, c2, s2m, s2p, qn_ref, qr_ref, *, n_nope, n_heads, rope_dim):
    qn_ref[...] = q_ref[:, :n_nope].astype(qn_ref.dtype)
    t2 = (c2[...], s2m[...], s2p[...])
    per = LANES // rope_dim
    for g in range(n_heads // per):
        r = _rope128(q_ref[:, n_nope + g * LANES:n_nope + (g + 1) * LANES], t2, rope_dim // 2)
        for u in range(per):
            qr_ref[g * per + u] = r[:, u * rope_dim:(u + 1) * rope_dim].astype(qr_ref.dtype)


def _post_q(q, tabs_mla, n_heads, rope_dim):
    m = q.shape[0]
    n_nope = n_heads * HEAD_DIM
    tm = _tile(m, (256, 128, 8))
    return pl.pallas_call(
        functools.partial(_post_q_body, n_nope=n_nope, n_heads=n_heads, rope_dim=rope_dim),
        grid=(m // tm,),
        in_specs=[pl.BlockSpec((tm, q.shape[1]), lambda i: (i, 0))] + [pl.BlockSpec((tm, LANES), lambda i: (i, 0))] * 3,
        out_specs=[pl.BlockSpec((tm, n_nope), lambda i: (i, 0)),
                   pl.BlockSpec((n_heads, tm, rope_dim), lambda i: (0, i, 0))],
        out_shape=[jax.ShapeDtypeStruct((m, n_nope), BF16), jax.ShapeDtypeStruct((n_heads, m, rope_dim), BF16)],
        compiler_params=_cparams(1),
        name="post_q",
    )(q, *tabs_mla)


def _post_odd_body(z_ref, c1, s1m, s1p, c3, s3m, s3p, q_ref, k_ref, v_ref, qi_ref, ki_ref, wi_ref,
                   *, offs, n_idx_heads, w_scale):
    o_q, o_k, o_v, o_qi, o_ki = offs
    t1 = (c1[...], s1m[...], s1p[...])
    t3 = (c3[...], s3m[...], s3p[...])
    for g in range((o_k - o_q) // LANES):
        q_ref[:, g * LANES:(g + 1) * LANES] = _rope128(
            z_ref[:, o_q + g * LANES:o_q + (g + 1) * LANES], t1, ROT_DIM // 2).astype(q_ref.dtype)
    k_ref[...] = _rope128(z_ref[:, o_k:o_v], t1, ROT_DIM // 2)
    v_ref[...] = z_ref[:, o_v:o_qi]
    idx_half = IDX_DIM // 4 // 2
    for g in range((o_ki - o_qi) // LANES):
        qi_ref[:, g * LANES:(g + 1) * LANES] = _rope128(
            z_ref[:, o_qi + g * LANES:o_qi + (g + 1) * LANES], t3, idx_half).astype(qi_ref.dtype)
    tail = z_ref[:, o_ki:o_ki + LANES]
    ki_ref[...] = _rope128(tail, t3, idx_half)[:, :IDX_DIM]
    wi_ref[...] = tail[:, IDX_DIM:IDX_DIM + n_idx_heads] * w_scale


def _post_odd(z, tabs_head, tabs_idx, offs, n_idx_heads):
    m, nz = z.shape
    o_q, o_k, o_v, o_qi, o_ki = offs
    tm = _tile(m, (256, 128, 8))
    row = lambda w: pl.BlockSpec((tm, w), lambda i: (i, 0))
    widths = (o_k - o_q, o_v - o_k, o_qi - o_v, o_ki - o_qi, IDX_DIM, n_idx_heads)
    dtypes = (BF16, F32, F32, BF16, F32, F32)
    w_scale = (n_idx_heads ** -0.5) * (IDX_DIM ** -0.5)
    return pl.pallas_call(
        functools.partial(_post_odd_body, offs=offs, n_idx_heads=n_idx_heads, w_scale=w_scale),
        grid=(m // tm,),
        in_specs=[row(nz)] + [row(LANES)] * 6,
        out_specs=[row(w) for w in widths],
        out_shape=[jax.ShapeDtypeStruct((m, w), d) for w, d in zip(widths, dtypes)],
        compiler_params=_cparams(1),
        name="post_odd",
    )(z, *tabs_head, *tabs_idx)


def _softmax_step(carry, s, v_bf):
    m, l, acc = carry
    m_new = jnp.maximum(m, jnp.max(s, axis=1, keepdims=True))
    a = jnp.exp(m - m_new)
    p = jnp.exp(s - m_new)
    l = a * l + jnp.sum(p, axis=1, keepdims=True)
    acc = a * acc + _dot(p.astype(BF16), v_bf)
    return m_new, l, acc


def _softmax_init(tq, dv):
    return (jnp.full((tq, 1), M_INIT, F32), jnp.zeros((tq, 1), F32), jnp.zeros((tq, dv), F32))


def _causal_keep(tq, tk, q0, k0):
    qpos = q0 + lax.broadcasted_iota(jnp.int32, (tq, tk), 0)
    kpos = k0 + lax.broadcasted_iota(jnp.int32, (tq, tk), 1)
    return kpos <= qpos


def _head_cols(u):
    return slice(u * HEAD_DIM, (u + 1) * HEAD_DIM)


def _write_heads(o_ref, carries):
    for u, (_, l, acc) in enumerate(carries):
        o_ref[:, _head_cols(u)] = (acc / l).astype(o_ref.dtype)


def _prompt_tiles(seq):
    return _tile(seq, (256, 128)), _tile(seq, (512, 256, 128))


def _mla_prompt_body(qn_ref, qr_ref, kn_ref, kr_ref, v_ref, o_ref, *, tq, tk, nh, scale):
    q0 = pl.program_id(2) * tq
    qn = [qn_ref[:, _head_cols(u)] for u in range(nh)]
    qr = [qr_ref[u] for u in range(nh)]

    def step(j, carry):
        k0 = pl.multiple_of(j * tk, tk)
        keep = _causal_keep(tq, tk, q0, k0)
        krb = kr_ref[pl.ds(k0, tk), :].astype(BF16)
        out = []
        for u in range(nh):
            s = _dot_nt(qn[u], kn_ref[pl.ds(k0, tk), _head_cols(u)]) + _dot_nt(qr[u], krb)
            s = jnp.where(keep, s * scale, NEG)
            out.append(_softmax_step(carry[u], s, v_ref[pl.ds(k0, tk), _head_cols(u)]))
        return tuple(out)

    init = tuple(_softmax_init(tq, HEAD_DIM) for _ in range(nh))
    _write_heads(o_ref, lax.fori_loop(0, (q0 + tq + tk - 1) // tk, step, init))


def _mla_prompt(qn, qr, kn, kr, v, batch, seq, n_heads, scale):
    tq, tk = _prompt_tiles(seq)
    nq = seq // tq
    nh = _tile(n_heads, (HEADS_PER_STEP, 2, 1))
    rope_dim = kr.shape[1]
    return pl.pallas_call(
        functools.partial(_mla_prompt_body, tq=tq, tk=tk, nh=nh, scale=scale),
        grid=(batch, n_heads // nh, nq),
        in_specs=[pl.BlockSpec((tq, nh * HEAD_DIM), lambda b, h, i: (b * nq + i, h)),
                  pl.BlockSpec((nh, tq, rope_dim), lambda b, h, i: (h, b * nq + i, 0)),
                  pl.BlockSpec((seq, nh * HEAD_DIM), lambda b, h, i: (b, h)),
                  pl.BlockSpec((seq, rope_dim), lambda b, h, i: (b, 0)),
                  pl.BlockSpec((seq, nh * HEAD_DIM), lambda b, h, i: (b, h))],
        out_specs=pl.BlockSpec((tq, nh * HEAD_DIM), lambda b, h, i: (b * nq + i, h)),
        out_shape=jax.ShapeDtypeStruct(qn.shape, BF16),
        compiler_params=_cparams(3),
        name="mla_prompt",
    )(qn, qr, kn, kr, v)


def _topk_lane_mask(gate, n_valid, n_top):
    lane = lax.broadcasted_iota(jnp.int32, gate.shape, 1)
    rank = jnp.zeros(gate.shape, F32)
    for m in range(n_valid):
        col = gate[:, m:m + 1]
        beats = (col > gate) | ((col == gate) & (m < lane))
        rank = rank + jnp.where(beats, 1.0, 0.0)
    return rank < n_top


def _moba_prompt_body(q_ref, k_ref, v_ref, o_ref, *, tq, blk, nh, n_blocks, n_top, scale):
    q0 = pl.program_id(2) * tq
    own = q0 // blk
    w_lanes = _round_up(n_blocks, SUBLANES)
    lane = lax.broadcasted_iota(jnp.int32, (tq, w_lanes), 1)
    k_means = [jnp.mean(k_ref[n * blk:(n + 1) * blk, :], axis=0, keepdims=True) for n in range(n_blocks)]
    k_means = jnp.concatenate(k_means + [jnp.zeros((w_lanes - n_blocks, HEAD_DIM), F32)] * (w_lanes > n_blocks), axis=0)
    qb, sel_bias = [], []
    for u in range(nh):
        q = q_ref[:, _head_cols(u)]
        qb.append(q.astype(BF16))
        gate = lax.dot_general(q, k_means, NT, precision=lax.Precision.HIGHEST, preferred_element_type=F32)
        gate = jnp.where(lane < own, gate, -jnp.inf)
        sel = _topk_lane_mask(gate, n_blocks, n_top) & (lane < own)
        sel_bias.append(jnp.where(sel, 0.0, NEG))

    def step(n, carry):
        k0 = pl.multiple_of(n * blk, blk)
        keep = _causal_keep(tq, blk, q0, k0)
        not_own = jnp.where(n == own, 0.0, 1.0)
        kb = k_ref[pl.ds(k0, blk), :].astype(BF16)
        vb = v_ref[pl.ds(k0, blk), :].astype(BF16)
        out = []
        for u in range(nh):
            col = jnp.sum(jnp.where(lane == n, sel_bias[u], 0.0), axis=1, keepdims=True)
            s = jnp.where(keep, _dot_nt(qb[u], kb) * scale + col * not_own, NEG)
            out.append(_softmax_step(carry[u], s, vb))
        return tuple(out)

    init = tuple(_softmax_init(tq, HEAD_DIM) for _ in range(nh))
    _write_heads(o_ref, lax.fori_loop(0, own + 1, step, init))


def _moba_prompt(q, k, v, batch, seq, n_heads, n_kv):
    blk = MOBA_BLOCK
    tq = min(blk, 256)
    nq = seq // tq
    n_blocks = seq // blk
    group = n_heads // n_kv
    nh = _tile(group, (HEADS_PER_STEP, 2, 1))
    steps_per_kv = group // nh
    return pl.pallas_call(
        functools.partial(_moba_prompt_body, tq=tq, blk=blk, nh=nh, n_blocks=n_blocks,
                          n_top=min(MOBA_TOPK, n_blocks), scale=HEAD_DIM ** -0.5),
        grid=(batch, n_heads // nh, nq),
        in_specs=[pl.BlockSpec((tq, nh * HEAD_DIM), lambda b, h, i: (b * nq + i, h)),
                  pl.BlockSpec((seq, HEAD_DIM), lambda b, h, i: (b, h // steps_per_kv)),
                  pl.BlockSpec((seq, HEAD_DIM), lambda b, h, i: (b, h // steps_per_kv))],
        out_specs=pl.BlockSpec((tq, nh * HEAD_DIM), lambda b, h, i: (b * nq + i, h)),
        out_shape=jax.ShapeDtypeStruct(q.shape, BF16),
        compiler_params=_cparams(3),
        name="moba_prompt",
    )(q, k, v)


def _sort_key(x):
    b = lax.bitcast_convert_type(x, jnp.int32)
    return jnp.where(b < 0, b ^ jnp.int32(0x7FFFFFFF), b)


def _kth_key(count_ge, rows, k):
    kf = jnp.float32(k)
    int_min = jnp.int32(-2 ** 31)
    t0 = jnp.where(count_ge(jnp.zeros((rows, 1), jnp.int32)) >= kf, jnp.int32(0), int_min)

    def step(it, t):
        cand = t | lax.shift_left(jnp.int32(1), jnp.int32(30) - it)
        return jnp.where(count_ge(cand) >= kf, cand, t)

    return lax.fori_loop(0, 31, step, jnp.broadcast_to(t0, (rows, 1)))


def _dsa_select_prompt_body(qi_ref, wi_ref, ki_ref, bias_ref, key_scr, *, tq, tk, n_chunks, n_idx_heads, n_keep):
    q0 = pl.program_id(1) * tq
    wi = wi_ref[...]
    for c in range(n_chunks):
        @pl.when(c * tk <= q0 + tq - 1)
        def _():
            kc = ki_ref[c * tk:(c + 1) * tk, :].astype(BF16)
            sc = jnp.zeros((tq, tk), F32)
            for h in range(n_idx_heads):
                logits = _dot_nt(qi_ref[:, h * IDX_DIM:(h + 1) * IDX_DIM], kc)
                sc = sc + wi[:, h:h + 1] * jnp.maximum(logits, 0.0)
            qpos = q0 + lax.broadcasted_iota(jnp.int32, (tq, tk), 0)
            kpos = c * tk + lax.broadcasted_iota(jnp.int32, (tq, tk), 1)
            key_scr[c] = _sort_key(jnp.where(kpos <= qpos, sc, -jnp.inf))

    n_live = (q0 + tq - 1) // tk + 1

    def count_ge(t):
        def add(c, a):
            return a + jnp.sum(jnp.where(key_scr[c] >= t, 1.0, 0.0), axis=1, keepdims=True)
        return lax.fori_loop(0, n_live, add, jnp.zeros((tq, 1), F32))

    thr = _kth_key(count_ge, tq, n_keep)
    for c in range(n_chunks):
        @pl.when(c * tk <= q0 + tq - 1)
        def _():
            qpos = q0 + lax.broadcasted_iota(jnp.int32, (tq, tk), 0)
            kpos = c * tk + lax.broadcasted_iota(jnp.int32, (tq, tk), 1)
            keep = (key_scr[c] >= thr) & (kpos <= qpos)
            bias_ref[0, c] = jnp.where(keep, 0.0, NEG).astype(bias_ref.dtype)

        @pl.when(c * tk > q0 + tq - 1)
        def _():
            bias_ref[0, c] = jnp.full((tq, tk), NEG, bias_ref.dtype)


def _dsa_select_prompt(qi, wi, ki, batch, seq, n_keep):
    tq, tk = _prompt_tiles(seq)
    nq = seq // tq
    n_chunks = seq // tk
    n_idx_heads = wi.shape[1]
    return pl.pallas_call(
        functools.partial(_dsa_select_prompt_body, tq=tq, tk=tk, n_chunks=n_chunks,
                          n_idx_heads=n_idx_heads, n_keep=n_keep),
        grid=(batch, nq),
        in_specs=[pl.BlockSpec((tq, qi.shape[1]), lambda b, i: (b * nq + i, 0)),
                  pl.BlockSpec((tq, n_idx_heads), lambda b, i: (b * nq + i, 0)),
                  pl.BlockSpec((seq, IDX_DIM), lambda b, i: (b, 0))],
        out_specs=pl.BlockSpec((1, n_chunks, tq, tk), lambda b, i: (b, 0, i, 0)),
        out_shape=jax.ShapeDtypeStruct((batch, n_chunks, seq, tk), BF16),
        scratch_shapes=[pltpu.VMEM((n_chunks, tq, tk), jnp.int32)],
        compiler_params=_cparams(2),
        name="dsa_select_prompt",
    )(qi, wi, ki)


def _dsa_prompt_body(q_ref, k_ref, v_ref, bias_ref, o_ref, *, tq, tk, nh, scale):
    q0 = pl.program_id(1) * tq
    q = [q_ref[:, _head_cols(u)] for u in range(nh)]

    def step(j, carry):
        k0 = pl.multiple_of(j * tk, tk)
        kb = k_ref[pl.ds(k0, tk), :].astype(BF16)
        vb = v_ref[pl.ds(k0, tk), :].astype(BF16)
        bias = bias_ref[0, j].astype(F32)
        return tuple(_softmax_step(carry[u], _dot_nt(q[u], kb) * scale + bias, vb) for u in range(nh))

    init = tuple(_softmax_init(tq, HEAD_DIM) for _ in range(nh))
    _write_heads(o_ref, lax.fori_loop(0, (q0 + tq - 1) // tk + 1, step, init))


def _dsa_prompt(q, k, v, bias, batch, seq, n_heads):
    n_chunks, tk = bias.shape[1], bias.shape[3]
    tq = _prompt_tiles(seq)[0]
    nq = seq // tq
    nh = _tile(n_heads, (HEADS_PER_STEP, 2, 1))
    return pl.pallas_call(
        functools.partial(_dsa_prompt_body, tq=tq, tk=tk, nh=nh, scale=HEAD_DIM ** -0.5),
        grid=(batch, nq, n_heads // nh),
        in_specs=[pl.BlockSpec((tq, nh * HEAD_DIM), lambda b, i, h: (b * nq + i, h)),
                  pl.BlockSpec((seq, HEAD_DIM), lambda b, i, h: (b, 0)),
                  pl.BlockSpec((seq, HEAD_DIM), lambda b, i, h: (b, 0)),
                  pl.BlockSpec((1, n_chunks, tq, tk), lambda b, i, h: (b, 0, i, 0))],
        out_specs=pl.BlockSpec((tq, nh * HEAD_DIM), lambda b, i, h: (b * nq + i, h)),
        out_shape=jax.ShapeDtypeStruct(q.shape, BF16),
        compiler_params=_cparams(3),
        name="dsa_prompt",
    )(q, k, v, bias)


def _page_specs(block_tail, layer, n_per_step):
    zeros = (0,) * len(block_tail)

    def spec(r):
        return pl.BlockSpec((1, 1) + block_tail, lambda s, g, pt: (layer, pt[s, g * n_per_step + r]) + zeros)

    return [spec(r) for r in range(n_per_step)]


def _mla_decode_body(pt_ref, ql_ref, qr_ref, cn_ref, krn_ref, *rest, n_pages_step, scale):
    lat_refs = rest[:n_pages_step]
    kr_refs = rest[n_pages_step:2 * n_pages_step]
    o_ref, m_scr, l_scr, acc_scr = rest[2 * n_pages_step:]
    g = pl.program_id(1)
    ql = ql_ref[0]
    qr = qr_ref[0]

    @pl.when(g == 0)
    def _():
        s_self = (jnp.sum(ql * cn_ref[0], axis=1, keepdims=True)
                  + jnp.sum(qr * krn_ref[0], axis=1, keepdims=True)) * scale
        m_scr[...] = s_self
        l_scr[...] = jnp.ones_like(s_self)
        acc_scr[...] = jnp.broadcast_to(cn_ref[0], acc_scr.shape)

    qlb, qrb = ql.astype(BF16), qr.astype(BF16)
    lats = [r[0, 0].astype(BF16) for r in lat_refs]
    s = jnp.concatenate([_dot_nt(qlb, lat) + _dot(qrb, kr[0, 0].astype(BF16))
                         for lat, kr in zip(lats, kr_refs)], axis=1) * scale
    m = m_scr[...]
    m_new = jnp.maximum(m, jnp.max(s, axis=1, keepdims=True))
    a = jnp.exp(m - m_new)
    p = jnp.exp(s - m_new)
    page = lats[0].shape[0]
    pv = sum(_dot(p[:, r * page:(r + 1) * page].astype(BF16), lat) for r, lat in enumerate(lats))
    m_scr[...] = m_new
    l_scr[...] = a * l_scr[...] + jnp.sum(p, axis=1, keepdims=True)
    acc_scr[...] = a * acc_scr[...] + pv

    @pl.when(g == pl.num_programs(1) - 1)
    def _():
        o_ref[0] = acc_scr[...] / l_scr[...]


def _mla_decode(q_lat, q_rope, c_new, kr_new, cache_lat, cache_kr_t, page_table, layer, scale):
    n_seq, n_heads, c_dim = q_lat.shape
    r_dim = q_rope.shape[2]
    page = cache_lat.shape[2]
    n_pages = page_table.shape[1]
    gp = _tile(n_pages, (PAGES_PER_STEP, 4, 2, 1))
    per_seq = lambda shape: pl.BlockSpec((1,) + shape, lambda s, g, pt: (s, 0, 0))
    grid_spec = pltpu.PrefetchScalarGridSpec(
        num_scalar_prefetch=1,
        grid=(n_seq, n_pages // gp),
        in_specs=[per_seq((n_heads, c_dim)), per_seq((n_heads, r_dim)), per_seq((1, c_dim)), per_seq((1, r_dim))]
        + _page_specs((page, c_dim), layer, gp) + _page_specs((r_dim, page), layer, gp),
        out_specs=per_seq((n_heads, c_dim)),
        scratch_shapes=[pltpu.VMEM((n_heads, 1), F32), pltpu.VMEM((n_heads, 1), F32),
                        pltpu.VMEM((n_heads, c_dim), F32)],
    )
    return pl.pallas_call(
        functools.partial(_mla_decode_body, n_pages_step=gp, scale=scale),
        grid_spec=grid_spec,
        out_shape=jax.ShapeDtypeStruct(q_lat.shape, F32),
        compiler_params=_cparams(2),
        name="mla_decode",
    )(page_table, q_lat, q_rope, c_new, kr_new, *([cache_lat] * gp), *([cache_kr_t] * gp))


def _moba_decode_body(pt_ref, q_ref, kn_ref, vn_ref, *rest, n_pages_step, pages_per_block, n_kv, n_top, scale):
    k_refs = rest[:n_pages_step]
    v_refs = rest[n_pages_step:2 * n_pages_step]
    o_ref, gate_scr, m_scr, l_scr, oblk_scr = rest[2 * n_pages_step:]
    g = pl.program_id(1)
    n_groups = pl.num_programs(1)
    q = q_ref[0]
    qb = q.astype(BF16)
    n_heads = q.shape[0]
    group = n_heads // n_kv
    head_kv = lax.broadcasted_iota(jnp.int32, (n_heads, 1), 0) // group
    lane = lax.broadcasted_iota(jnp.int32, gate_scr.shape, 1)
    blocks_step = n_pages_step // pages_per_block

    @pl.when(g == 0)
    def _():
        for scr in (gate_scr, m_scr, l_scr):
            scr[...] = jnp.zeros(scr.shape, F32)

    def per_kv(vals):
        out = vals[0]
        for kh in range(1, n_kv):
            out = jnp.where(head_kv == kh, vals[kh], out)
        return out

    rows_blk = pages_per_block * k_refs[0].shape[2]
    row_kv = lax.broadcasted_iota(jnp.int32, (n_heads, rows_blk), 1) & (n_kv - 1)
    own_rows = row_kv == head_kv
    sub_kv = lax.broadcasted_iota(jnp.int32, (SUBLANES, HEAD_DIM), 0) & (n_kv - 1)
    for jb in range(blocks_step):
        n = g * blocks_step + jb
        refs = range(jb * pages_per_block, (jb + 1) * pages_per_block)
        k_blk = jnp.concatenate([k_refs[r][0, 0] for r in refs], axis=0)
        v_blk = jnp.concatenate([v_refs[r][0, 0] for r in refs], axis=0).astype(BF16)
        k_sum = jnp.sum(k_blk.reshape(rows_blk // SUBLANES, SUBLANES, HEAD_DIM), axis=0)
        gates = []
        for kh in range(n_kv):
            k_mean = jnp.sum(jnp.where(sub_kv == kh, k_sum, 0.0), axis=0, keepdims=True) * (n_kv / rows_blk)
            gates.append(jnp.sum(q * k_mean, axis=1, keepdims=True))
        s = jnp.where(own_rows, _dot_nt(qb, k_blk.astype(BF16)) * scale, NEG)
        m_n = jnp.max(s, axis=1, keepdims=True)
        p = jnp.exp(s - m_n)
        gate_scr[...] = jnp.where(lane == n, per_kv(gates), gate_scr[...])
        m_scr[...] = jnp.where(lane == n, m_n, m_scr[...])
        l_scr[...] = jnp.where(lane == n, jnp.sum(p, axis=1, keepdims=True), l_scr[...])
        oblk_scr[n] = _dot(p.astype(BF16), v_blk)

    @pl.when(g == n_groups - 1)
    def _():
        n_past = n_groups * blocks_step
        in_past = lane < n_past
        gate = jnp.where(in_past, gate_scr[...], -jnp.inf)
        sel = _topk_lane_mask(gate, n_past, n_top) & in_past
        k_self = per_kv([kn_ref[0, :, kh, :] for kh in range(n_kv)])
        v_self = per_kv([vn_ref[0, :, kh, :] for kh in range(n_kv)])
        s_self = jnp.sum(q * k_self, axis=1, keepdims=True) * scale
        m_blk = jnp.where(sel, m_scr[...], NEG)
        m_tot = jnp.maximum(jnp.max(m_blk, axis=1, keepdims=True), s_self)
        w_blk = jnp.where(sel, jnp.exp(m_blk - m_tot), 0.0)
        w_self = jnp.exp(s_self - m_tot)
        l_tot = jnp.sum(w_blk * l_scr[...], axis=1, keepdims=True) + w_self
        o = w_self * v_self
        for nb in range(n_past):
            o = o + w_blk[:, nb:nb + 1] * oblk_scr[nb]
        o_ref[0] = o / l_tot


def _moba_decode(q, k_new, v_new, cache_k, cache_v, page_table, layer):
    n_seq, n_heads, _ = q.shape
    n_layers, pool, page, n_kv, _ = cache_k.shape
    n_pages = page_table.shape[1]
    ppb = MOBA_BLOCK // page
    gp = _tile(n_pages, (PAGES_PER_STEP, 4, 2, 1))
    assert MOBA_BLOCK % page == 0 and gp % ppb == 0 and (n_pages * page) % MOBA_BLOCK == 0
    assert n_kv & (n_kv - 1) == 0 and SUBLANES % n_kv == 0
    n_past = n_pages // ppb
    assert n_past <= LANES
    cache_k = cache_k.reshape(n_layers, pool, page * n_kv, HEAD_DIM)
    cache_v = cache_v.reshape(n_layers, pool, page * n_kv, HEAD_DIM)
    per_seq = lambda shape: pl.BlockSpec((1,) + shape, lambda s, g, pt: (s,) + (0,) * len(shape))
    grid_spec = pltpu.PrefetchScalarGridSpec(
        num_scalar_prefetch=1,
        grid=(n_seq, n_pages // gp),
        in_specs=[per_seq((n_heads, HEAD_DIM)), per_seq((1, n_kv, HEAD_DIM)), per_seq((1, n_kv, HEAD_DIM))]
        + _page_specs((page * n_kv, HEAD_DIM), layer, gp) + _page_specs((page * n_kv, HEAD_DIM), layer, gp),
        out_specs=per_seq((n_heads, HEAD_DIM)),
        scratch_shapes=[pltpu.VMEM((n_heads, LANES), F32)] * 3 + [pltpu.VMEM((n_past, n_heads, HEAD_DIM), F32)],
    )
    return pl.pallas_call(
        functools.partial(_moba_decode_body, n_pages_step=gp, pages_per_block=ppb, n_kv=n_kv,
                          n_top=min(MOBA_TOPK, n_past + 1), scale=HEAD_DIM ** -0.5),
        grid_spec=grid_spec,
        out_shape=jax.ShapeDtypeStruct(q.shape, F32),
        compiler_params=_cparams(2),
        name="moba_decode",
    )(page_table, q, k_new, v_new, *([cache_k] * gp), *([cache_v] * gp))


def _dsa_score_decode_body(pt_ref, qi_ref, wi_ref, kin_ref, *rest, n_pages_step):
    ki_refs = rest[:n_pages_step]
    o_ref = rest[n_pages_step]
    g = pl.program_id(1)
    qi = qi_ref[0]
    wi = wi_ref[0]

    @pl.when(g < pl.num_programs(1) - 1)
    def _():
        qb = qi.astype(BF16)
        logits = jnp.concatenate([_dot(qb, r[0, 0].astype(BF16)) for r in ki_refs], axis=1)
        o_ref[0] = jnp.sum(wi * jnp.maximum(logits, 0.0), axis=0, keepdims=True)

    @pl.when(g == pl.num_programs(1) - 1)
    def _():
        logit = jnp.sum(qi * kin_ref[0], axis=1, keepdims=True)
        sc = jnp.sum(wi * jnp.maximum(logit, 0.0), axis=0, keepdims=True)
        lane = lax.broadcasted_iota(jnp.int32, o_ref.shape[1:], 1)
        o_ref[0] = jnp.where(lane == 0, sc, -jnp.inf)


def _dsa_score_decode(qi, wi, ki_new, cache_ki_t, page_table, layer):
    n_seq, n_ih, _ = qi.shape
    page = cache_ki_t.shape[3]
    n_pages = page_table.shape[1]
    gp = _tile(n_pages, (SCORE_PAGES_PER_STEP, PAGES_PER_STEP, 4, 2, 1))
    n_groups = n_pages // gp
    per_seq = lambda shape: pl.BlockSpec((1,) + shape, lambda s, g, pt: (s, 0, 0))
    zeros = (0, 0)

    def page_spec(r):
        return pl.BlockSpec((1, 1, IDX_DIM, page),
                            lambda s, g, pt: (layer, pt[s, jnp.minimum(g, n_groups - 1) * gp + r]) + zeros)

    grid_spec = pltpu.PrefetchScalarGridSpec(
        num_scalar_prefetch=1,
        grid=(n_seq, n_groups + 1),
        in_specs=[per_seq((n_ih, IDX_DIM)), per_seq((n_ih, 1)), per_seq((1, IDX_DIM))]
        + [page_spec(r) for r in range(gp)],
        out_specs=pl.BlockSpec((1, 1, gp * page), lambda s, g, pt: (s, 0, g)),
    )
    return pl.pallas_call(
        functools.partial(_dsa_score_decode_body, n_pages_step=gp),
        grid_spec=grid_spec,
        out_shape=jax.ShapeDtypeStruct((n_seq, 1, (n_groups + 1) * gp * page), F32),
        compiler_params=_cparams(2),
        name="dsa_score_decode",
    )(page_table, qi, wi, ki_new, *([cache_ki_t] * gp))


def _dsa_select_decode_body(sc_ref, bias_ref, *, n_keys, n_keep):
    key = _sort_key(sc_ref[...])
    rows = key.shape[0]
    thr = _kth_key(lambda t: jnp.sum(jnp.where(key >= t, 1.0, 0.0), axis=1, keepdims=True), rows, n_keep)
    lane = lax.broadcasted_iota(jnp.int32, key.shape, 1)
    bias_ref[...] = jnp.where((key >= thr) & (lane < n_keys), 0.0, NEG)


def _dsa_select_decode(scores, n_keys, n_keep):
    return pl.pallas_call(
        functools.partial(_dsa_select_decode_body, n_keys=n_keys, n_keep=n_keep),
        out_shape=jax.ShapeDtypeStruct(scores.shape, F32),
        compiler_params=pltpu.CompilerParams(vmem_limit_bytes=VMEM_LIMIT),
        name="dsa_select_decode",
    )(scores)


def _dsa_decode_body(pt_ref, q_ref, kn_ref, vn_ref, bias_ref, bself_ref, *rest, n_pages_step, scale):
    k_refs = rest[:n_pages_step]
    v_refs = rest[n_pages_step:2 * n_pages_step]
    o_ref, m_scr, l_scr, acc_scr = rest[2 * n_pages_step:]
    g = pl.program_id(1)
    q = q_ref[0]

    @pl.when(g == 0)
    def _():
        s_self = jnp.sum(q * kn_ref[0], axis=1, keepdims=True) * scale + bself_ref[0][:, 0:1]
        m_scr[...] = jnp.maximum(s_self, M_INIT)
        p_self = jnp.exp(s_self - m_scr[...])
        l_scr[...] = p_self
        acc_scr[...] = p_self * vn_ref[0]

    qb = q.astype(BF16)
    s = jnp.concatenate([_dot_nt(qb, r[0, 0].astype(BF16)) for r in k_refs], axis=1) * scale + bias_ref[0]
    m = m_scr[...]
    m_new = jnp.maximum(m, jnp.max(s, axis=1, keepdims=True))
    a = jnp.exp(m - m_new)
    p = jnp.exp(s - m_new)
    page = k_refs[0].shape[2]
    pv = sum(_dot(p[:, r * page:(r + 1) * page].astype(BF16), v[0, 0].astype(BF16)) for r, v in enumerate(v_refs))
    m_scr[...] = m_new
    l_scr[...] = a * l_scr[...] + jnp.sum(p, axis=1, keepdims=True)
    acc_scr[...] = a * acc_scr[...] + pv

    @pl.when(g == pl.num_programs(1) - 1)
    def _():
        o_ref[0] = acc_scr[...] / l_scr[...]


def _dsa_decode(q, k_new, v_new, bias, cache_k, cache_v, page_table, layer):
    n_seq, n_heads, _ = q.shape
    page = cache_k.shape[2]
    n_pages = page_table.shape[1]
    gp = _tile(n_pages, (PAGES_PER_STEP, 4, 2, 1))
    n_groups = n_pages // gp
    per_seq = lambda shape: pl.BlockSpec((1,) + shape, lambda s, g, pt: (s, 0, 0))
    grid_spec = pltpu.PrefetchScalarGridSpec(
        num_scalar_prefetch=1,
        grid=(n_seq, n_groups),
        in_specs=[per_seq((n_heads, HEAD_DIM)), per_seq((1, HEAD_DIM)), per_seq((1, HEAD_DIM)),
                  pl.BlockSpec((1, 1, gp * page), lambda s, g, pt: (s, 0, g)),
                  pl.BlockSpec((1, 1, gp * page), lambda s, g, pt: (s, 0, n_groups))]
        + _page_specs((page, HEAD_DIM), layer, gp) + _page_specs((page, HEAD_DIM), layer, gp),
        out_specs=per_seq((n_heads, HEAD_DIM)),
        scratch_shapes=[pltpu.VMEM((n_heads, 1), F32), pltpu.VMEM((n_heads, 1), F32),
                        pltpu.VMEM((n_heads, HEAD_DIM), F32)],
    )
    return pl.pallas_call(
        functools.partial(_dsa_decode_body, n_pages_step=gp, scale=HEAD_DIM ** -0.5),
        grid_spec=grid_spec,
        out_shape=jax.ShapeDtypeStruct(q.shape, F32),
        compiler_params=_cparams(2),
        name="dsa_decode",
    )(page_table, q, k_new, v_new, bias, bias, *([cache_k] * gp), *([cache_v] * gp))


def _silu_gate(u_conv, g):
    return u_conv / (1.0 + jnp.exp(-u_conv)) * g


def _ffn_up_prompt_body(x_ref, wu_ref, wg_ref, cw_ref, cb_ref, act_ref, tail_ref, u_scr, *, tm, tiles_per_seq):
    i = pl.program_id(1)

    @pl.when(i % tiles_per_seq == 0)
    def _():
        u_scr[0:SUBLANES, :] = jnp.zeros((SUBLANES, u_scr.shape[1]), F32)

    x = x_ref[...]
    u = _dot(x, wu_ref[...])
    g = _dot(x, wg_ref[...])
    u_scr[SUBLANES:SUBLANES + tm, :] = u
    u_conv = (cb_ref[...] + cw_ref[0:1, :] * u_scr[SUBLANES - 2:SUBLANES - 2 + tm, :]
              + cw_ref[1:2, :] * u_scr[SUBLANES - 1:SUBLANES - 1 + tm, :] + cw_ref[2:3, :] * u)
    act_ref[...] = _silu_gate(u_conv, g).astype(act_ref.dtype)
    tail = u_scr[tm:tm + SUBLANES, :]
    u_scr[0:SUBLANES, :] = tail
    tail_ref[0] = tail


def _ffn_up_prompt(x, w_in, conv_w, conv_b, batch, seq):
    m, d = x.shape
    f = w_in.shape[1] // 2
    tm = _tile(seq, (1024, 512, 256, 128))
    tf = _tile(f, (512, 256, 128))
    nf = f // tf
    tps = seq // tm
    return pl.pallas_call(
        functools.partial(_ffn_up_prompt_body, tm=tm, tiles_per_seq=tps),
        grid=(nf, m // tm),
        in_specs=[pl.BlockSpec((tm, d), lambda j, i: (i, 0)),
                  pl.BlockSpec((d, tf), lambda j, i: (0, j)),
                  pl.BlockSpec((d, tf), lambda j, i: (0, j + nf)),
                  pl.BlockSpec((conv_w.shape[0], tf), lambda j, i: (0, j)),
                  pl.BlockSpec((1, tf), lambda j, i: (0, j))],
        out_specs=[pl.BlockSpec((tm, tf), lambda j, i: (i, j)),
                   pl.BlockSpec((1, SUBLANES, tf), lambda j, i: (i // tps, 0, j))],
        out_shape=[jax.ShapeDtypeStruct((m, f), BF16), jax.ShapeDtypeStruct((batch, SUBLANES, f), F32)],
        scratch_shapes=[pltpu.VMEM((tm + SUBLANES, tf), F32)],
        compiler_params=_cparams(2),
        name="ffn_up_prompt",
    )(x, w_in, w_in, conv_w, conv_b.reshape(1, f))


def _ffn_up_decode_body(x_ref, wu_ref, wg_ref, cw_ref, cb_ref, p2_ref, p1_ref, act_ref, u_ref):
    x = x_ref[...]
    u = _dot(x, wu_ref[...])
    g = _dot(x, wg_ref[...])
    u_conv = cb_ref[...] + cw_ref[0:1, :] * p2_ref[...] + cw_ref[1:2, :] * p1_ref[...] + cw_ref[2:3, :] * u
    act_ref[...] = _silu_gate(u_conv, g).astype(act_ref.dtype)
    u_ref[...] = u


def _ffn_up_decode(x, w_in, conv_w, conv_b, prev2, prev1):
    m, d = x.shape
    f = w_in.shape[1] // 2
    tf = _tile(f, (512, 256, 128))
    nf = f // tf
    col = lambda rows: pl.BlockSpec((rows, tf), lambda j: (0, j))
    return pl.pallas_call(
        _ffn_up_decode_body,
        grid=(nf,),
        in_specs=[pl.BlockSpec((m, d), lambda j: (0, 0)), pl.BlockSpec((d, tf), lambda j: (0, j)),
                  pl.BlockSpec((d, tf), lambda j: (0, j + nf)), col(conv_w.shape[0]), col(1), col(m), col(m)],
        out_specs=[col(m), col(m)],
        out_shape=[jax.ShapeDtypeStruct((m, f), BF16), jax.ShapeDtypeStruct((m, f), F32)],
        compiler_params=_cparams(1),
        name="ffn_up_decode",
    )(x, w_in, w_in, conv_w, conv_b.reshape(1, f), prev2, prev1)


def _pad_cols(w, n):
    return jnp.pad(w, ((0, 0), (0, n - w.shape[1])))


def _even_weights(w_in, w_q_up, w_uk, w_uv, w_out, q_lora, kv_lora, rope_dim, n_moba, n_kv):
    sizes = (q_lora, kv_lora, rope_dim, n_moba * HEAD_DIM, n_kv * HEAD_DIM, n_kv * HEAD_DIM)
    cuts = [0]
    for s in sizes:
        cuts.append(cuts[-1] + s)
    cq, ckv, kr, mq, mk, mv = [w_in[:, a:b] for a, b in zip(cuts[:-1], cuts[1:])]
    order = (cq, ckv, mq, mk, mv, kr)
    offs = [0]
    for p in order:
        offs.append(offs[-1] + p.shape[1])
    n_pad = _round_up(offs[-2] + LANES, 2 * LANES)
    w_in_r = _pad_cols(jnp.concatenate(order, axis=1), n_pad).astype(BF16)
    n_heads = w_uk.shape[1]
    nope = w_uk.shape[2]
    wq = w_q_up.reshape(q_lora, n_heads, nope + rope_dim)
    wq_r = jnp.concatenate([wq[:, :, :nope].reshape(q_lora, -1), wq[:, :, nope:].reshape(q_lora, -1)], axis=1).astype(BF16)
    return dict(w_in=w_in_r, offs=tuple(offs), w_q=wq_r,
                w_uk_flat=w_uk.reshape(kv_lora, -1).astype(BF16), w_uv_flat=w_uv.reshape(kv_lora, -1).astype(BF16),
                w_uk_t=jnp.transpose(w_uk, (1, 2, 0)).astype(BF16), w_uv_h=jnp.transpose(w_uv, (1, 0, 2)).astype(BF16),
                w_out=w_out.astype(BF16))


def _even_project(hn, ew, q_norm, kv_norm, tabs_head, tabs_mla, rope_dim, n_mla):
    z = _mm(hn, ew["w_in"])
    cqn, ckv, kr, mq, mk, mv = _post_even(z, q_norm, kv_norm, tabs_head, tabs_mla, ew["offs"], rope_dim)
    qn, qr = _post_q(_mm(cqn, ew["w_q"]), tabs_mla, n_mla, rope_dim)
    return qn, qr, ckv, kr, mq, mk, mv


def _ffn(h, hn, w_in, conv_w, conv_b, w_out, prompt_dims=None, state=None):
    if prompt_dims is not None:
        act, tail = _ffn_up_prompt(hn, w_in, conv_w, conv_b, *prompt_dims)
        new_state = tail[:, SUBLANES - (conv_w.shape[0] - 1):, :]
    else:
        act, u = _ffn_up_decode(hn, w_in, conv_w, conv_b, state[:, 0, :], state[:, 1, :])
        new_state = jnp.stack([state[:, 1, :], u], axis=1)
    return _mm(act, w_out, res=h), new_state


def kernel(x_prompt, x_sample, cache_mla_latent, cache_mla_krope, cache_moba_k, cache_moba_v, cache_dsa_k, cache_dsa_v, cache_dsa_kidx, state_ffn_conv, page_table, norm_attn, norm_ffn, norm_final, w_in_ab, mla_q_norm, mla_w_q_up, mla_kv_norm, mla_w_uk, mla_w_uv, w_out_ab, w_in_c, w_out_c, w_ffn_in, conv_w, conv_b, w_ffn_out):
    batch, seq, d_model = x_prompt.shape
    n_dec, dec_seq, _ = x_sample.shape
    depth = norm_attn.shape[0]
    assert dec_seq == 1 and conv_w.shape[1] == 3 and seq & (seq - 1) == 0
    page = cache_mla_latent.shape[2]
    past_len = page_table.shape[1] * page
    q_lora, kv_lora, rope_dim = mla_q_norm.shape[1], mla_kv_norm.shape[1], cache_mla_krope.shape[3]
    n_mla, nope = mla_w_uk.shape[2], mla_w_uk.shape[3]
    n_kv = cache_moba_k.shape[3]
    n_moba = (w_in_ab.shape[2] - q_lora - kv_lora - rope_dim) // HEAD_DIM - 2 * n_kv
    n_dsa = w_out_c.shape[1] // HEAD_DIM
    n_idx = (w_in_c.shape[2] - (n_dsa + 2) * HEAD_DIM - IDX_DIM) // (IDX_DIM + 1)
    assert nope == HEAD_DIM and mla_w_uv.shape[3] == HEAD_DIM and cache_dsa_kidx.shape[3] == IDX_DIM
    tp = batch * seq
    mla_scale = 1.0 / math.sqrt(nope + rope_dim)

    pos_p = jnp.tile(jnp.arange(seq, dtype=jnp.int32), batch)
    pos_s = jnp.full((n_dec,), past_len, jnp.int32)
    tabs = {}
    for name, pos in (("p", pos_p), ("s", pos_s)):
        tabs[name] = dict(head=_rope_tables(pos, ROT_DIM, HEAD_DIM), mla=_rope_tables(pos, rope_dim, rope_dim),
                          idx=_rope_tables(pos, IDX_DIM // 4, IDX_DIM))

    cache_kr_t = jnp.swapaxes(cache_mla_krope, 2, 3)
    cache_ki_t = jnp.swapaxes(cache_dsa_kidx, 2, 3)

    h_p = x_prompt.reshape(tp, d_model)
    h_s = x_sample.reshape(n_dec, d_model)
    outs = {k: [] for k in ("lat_p", "lat_s", "kr_p", "kr_s", "mk_p", "mk_s", "mv_p", "mv_s",
                            "dk_p", "dk_s", "dv_p", "dv_s", "dki_p", "dki_s", "cv_p", "cv_s")}

    for layer in range(depth):
        i = layer // 2
        hn_p = _rmsnorm(h_p, norm_attn[layer], BF16)
        hn_s = _rmsnorm(h_s, norm_attn[layer], BF16)
        if layer % 2 == 0:
            ew = _even_weights(w_in_ab[i], mla_w_q_up[i], mla_w_uk[i], mla_w_uv[i], w_out_ab[i],
                               q_lora, kv_lora, rope_dim, n_moba, n_kv)
            qn, qr, ckv, kr, mq, mk, mv = _even_project(hn_p, ew, mla_q_norm[i], mla_kv_norm[i],
                                                        tabs["p"]["head"], tabs["p"]["mla"], rope_dim, n_mla)
            ckv_b = ckv.astype(BF16)
            o_mla = _mla_prompt(qn, qr, _mm(ckv_b, ew["w_uk_flat"], out_dtype=BF16), kr,
                                _mm(ckv_b, ew["w_uv_flat"], out_dtype=BF16), batch, seq, n_mla, mla_scale)
            o_moba = _moba_prompt(mq, mk, mv, batch, seq, n_moba, n_kv)
            h_p = _mm(jnp.concatenate([o_mla, o_moba], axis=1), ew["w_out"], res=h_p)
            outs["lat_p"].append(ckv.reshape(batch, seq, kv_lora))
            outs["kr_p"].append(kr.reshape(batch, seq, rope_dim))
            outs["mk_p"].append(mk.reshape(batch, seq, n_kv, HEAD_DIM))
            outs["mv_p"].append(mv.reshape(batch, seq, n_kv, HEAD_DIM))
            qn, qr, ckv, kr, mq, mk, mv = _even_project(hn_s, ew, mla_q_norm[i], mla_kv_norm[i],
                                                        tabs["s"]["head"], tabs["s"]["mla"], rope_dim, n_mla)
            q_lat = _bmm_heads(qn, ew["w_uk_t"]).reshape(n_dec, n_mla, kv_lora)
            o_lat = _mla_decode(q_lat, jnp.transpose(qr, (1, 0, 2)).astype(F32), ckv.reshape(n_dec, 1, kv_lora),
                                kr.reshape(n_dec, 1, rope_dim), cache_mla_latent, cache_kr_t,
                                page_table, i, mla_scale)
            o_mla = _bmm_heads(o_lat.reshape(n_dec, n_mla * kv_lora), ew["w_uv_h"])
            o_moba = _moba_decode(mq.reshape(n_dec, n_moba, HEAD_DIM), mk.reshape(n_dec, 1, n_kv, HEAD_DIM),
                                  mv.reshape(n_dec, 1, n_kv, HEAD_DIM), cache_moba_k, cache_moba_v, page_table, i)
            o_cat = jnp.concatenate([o_mla, o_moba.reshape(n_dec, n_moba * HEAD_DIM)], axis=1).astype(BF16)
            h_s = _mm(o_cat, ew["w_out"], res=h_s)
            outs["lat_s"].append(ckv.reshape(n_dec, 1, kv_lora))
            outs["kr_s"].append(kr.reshape(n_dec, 1, rope_dim))
            outs["mk_s"].append(mk.reshape(n_dec, 1, n_kv, HEAD_DIM))
            outs["mv_s"].append(mv.reshape(n_dec, 1, n_kv, HEAD_DIM))
        else:
            e_in = w_in_c.shape[2]
            w_in = _pad_cols(w_in_c[i], _round_up(e_in - IDX_DIM - n_idx + LANES, 2 * LANES)).astype(BF16)
            w_out = w_out_c[i].astype(BF16)
            o_k = n_dsa * HEAD_DIM
            offs = (0, o_k, o_k + HEAD_DIM, o_k + 2 * HEAD_DIM, o_k + 2 * HEAD_DIM + n_idx * IDX_DIM)
            q, k, v, qi, ki, wi = _post_odd(_mm(hn_p, w_in), tabs["p"]["head"], tabs["p"]["idx"], offs, n_idx)
            bias = _dsa_select_prompt(qi, wi, ki, batch, seq, min(DSA_TOPK, seq // 4))
            o = _dsa_prompt(q, k, v, bias, batch, seq, n_dsa)
            h_p = _mm(o, w_out, res=h_p)
            outs["dk_p"].append(k.reshape(batch, seq, HEAD_DIM))
            outs["dv_p"].append(v.reshape(batch, seq, HEAD_DIM))
            outs["dki_p"].append(ki.reshape(batch, seq, IDX_DIM))
            q, k, v, qi, ki, wi = _post_odd(_mm(hn_s, w_in), tabs["s"]["head"], tabs["s"]["idx"], offs, n_idx)
            scores = _dsa_score_decode(qi.astype(F32).reshape(n_dec, n_idx, IDX_DIM), wi.reshape(n_dec, n_idx, 1),
                                       ki.reshape(n_dec, 1, IDX_DIM), cache_ki_t, page_table, i)
            n_keys = past_len + 1
            bias = _dsa_select_decode(scores.reshape(n_dec, -1), n_keys, min(DSA_TOPK, n_keys // 4))
            o = _dsa_decode(q.astype(F32).reshape(n_dec, n_dsa, HEAD_DIM), k.reshape(n_dec, 1, HEAD_DIM),
                            v.reshape(n_dec, 1, HEAD_DIM), bias.reshape(n_dec, 1, -1),
                            cache_dsa_k, cache_dsa_v, page_table, i)
            h_s = _mm(o.reshape(n_dec, n_dsa * HEAD_DIM).astype(BF16), w_out, res=h_s)
            outs["dk_s"].append(k.reshape(n_dec, 1, HEAD_DIM))
            outs["dv_s"].append(v.reshape(n_dec, 1, HEAD_DIM))
            outs["dki_s"].append(ki.reshape(n_dec, 1, IDX_DIM))

        w_fi = w_ffn_in[layer].astype(BF16)
        w_fo = w_ffn_out[layer].astype(BF16)
        h_p, st_p = _ffn(h_p, _rmsnorm(h_p, norm_ffn[layer], BF16), w_fi, conv_w[layer], conv_b[layer], w_fo,
                         prompt_dims=(batch, seq))
        h_s, st_s = _ffn(h_s, _rmsnorm(h_s, norm_ffn[layer], BF16), w_fi, conv_w[layer], conv_b[layer], w_fo,
                         state=state_ffn_conv[layer])
        outs["cv_p"].append(st_p)
        outs["cv_s"].append(st_s)

    y_p = _rmsnorm(h_p, norm_final, F32).reshape(batch, seq, d_model)
    y_s = _rmsnorm(h_s, norm_final, F32).reshape(n_dec, 1, d_model)
    order = ("lat_p", "lat_s", "kr_p", "kr_s", "mk_p", "mk_s", "mv_p", "mv_s",
             "dk_p", "dk_s", "dv_p", "dv_s", "dki_p", "dki_s", "cv_p", "cv_s")
    return (y_p, y_s) + tuple(jnp.stack(outs[k]) for k in order)
```

```python
import functools
import math

import jax
import jax.numpy as jnp
from jax import lax
from jax.experimental import pallas as pl
from jax.experimental.pallas import tpu as pltpu

HEAD_DIM = 128
ROT_DIM = HEAD_DIM // 4
ROPE_THETA = 500000.0
RMS_EPS = 1e-6
MOBA_BLOCK = 256
MOBA_TOPK = 3
DSA_TOPK = 256
IDX_DIM = 64
NEG = -1e30
M_INIT = -1e29
LANES = 128
SUBLANES = 8
VMEM_LIMIT = 56 * 1024 * 1024
MM_VMEM_BUDGET = 46 * 1024 * 1024
MM_K_SPLIT = 4096
PAGES_PER_STEP = 16
SCORE_PAGES_PER_STEP = 32
HEADS_PER_STEP = 8
MLA_HEADS_PER_STEP = 4

F32 = jnp.float32
BF16 = jnp.bfloat16
NT = (((1,), (1,)), ((), ()))


def _cparams(n_axes):
    return pltpu.CompilerParams(dimension_semantics=("arbitrary",) * n_axes, vmem_limit_bytes=VMEM_LIMIT)


def _tile(n, cands):
    for c in cands:
        if n % c == 0:
            return c
    return n


def _round_up(n, m):
    return -(-n // m) * m


def _dot(a, b):
    return jnp.dot(a, b, preferred_element_type=F32)


def _dot_nt(a, b):
    return lax.dot_general(a, b, NT, preferred_element_type=F32)


def _mm_body(*refs, has_res, n_k):
    if has_res:
        x_ref, w_ref, r_ref, o_ref = refs[:4]
    else:
        x_ref, w_ref, o_ref = refs[:3]
    part = _dot(x_ref[...], w_ref[...])

    def finish(acc):
        if has_res:
            acc = acc + r_ref[...]
        o_ref[...] = acc.astype(o_ref.dtype)

    if n_k == 1:
        finish(part)
        return
    acc_ref = refs[-1]
    kk = pl.program_id(2)

    @pl.when(kk == 0)
    def _():
        acc_ref[...] = part

    @pl.when((kk > 0) & (kk < n_k - 1))
    def _():
        acc_ref[...] += part

    @pl.when(kk == n_k - 1)
    def _():
        finish(acc_ref[...] + part)


def _mm_tiles(m, k, n):
    for tm in (1024, 512, 256, 128):
        if m % tm:
            continue
        for tk in (k, MM_K_SPLIT):
            if k % tk:
                continue
            for tn in (512, 256, 128):
                if n % tn:
                    continue
                if 2 * (tm * tk * 2 + tk * tn * 2 + 2 * tm * tn * 4) + tm * tn * 4 <= MM_VMEM_BUDGET:
                    return tm, tn, tk
    return _tile(m, (128, 8)), _tile(n, (128,)), k


def _mm(x, w, res=None, out_dtype=F32):
    m, k = x.shape
    n = w.shape[1]
    tm, tn, tk = _mm_tiles(m, k, n)
    n_k = k // tk
    in_specs = [pl.BlockSpec((tm, tk), lambda i, j, kk: (i, kk)), pl.BlockSpec((tk, tn), lambda i, j, kk: (kk, j))]
    args = [x, w]
    if res is not None:
        in_specs.append(pl.BlockSpec((tm, tn), lambda i, j, kk: (i, j)))
        args.append(res)
    return pl.pallas_call(
        functools.partial(_mm_body, has_res=res is not None, n_k=n_k),
        grid=(m // tm, n // tn, n_k),
        in_specs=in_specs,
        out_specs=pl.BlockSpec((tm, tn), lambda i, j, kk: (i, j)),
        out_shape=jax.ShapeDtypeStruct((m, n), out_dtype),
        scratch_shapes=[pltpu.VMEM((tm, tn), F32)] if n_k > 1 else [],
        compiler_params=_cparams(3),
        name="mm",
    )(*args)


def _bmm_body(x_ref, w_ref, o_ref):
    o_ref[...] = _dot(x_ref[...].astype(BF16), w_ref[0]).astype(o_ref.dtype)


def _bmm_heads(x, w):
    r = x.shape[0]
    h, k, n = w.shape
    return pl.pallas_call(
        _bmm_body,
        grid=(h,),
        in_specs=[pl.BlockSpec((r, k), lambda i: (0, i)), pl.BlockSpec((1, k, n), lambda i: (i, 0, 0))],
        out_specs=pl.BlockSpec((r, n), lambda i: (0, i)),
        out_shape=jax.ShapeDtypeStruct((r, h * n), F32),
        compiler_params=_cparams(1),
        name="bmm_heads",
    )(x, w)


def _rms(x, g):
    return x * lax.rsqrt(jnp.mean(x * x, axis=-1, keepdims=True) + RMS_EPS) * g


def _rms_body(x_ref, g_ref, o_ref):
    o_ref[...] = _rms(x_ref[...], g_ref[...]).astype(o_ref.dtype)


def _rmsnorm(x, g, out_dtype):
    m, d = x.shape
    tm = _tile(m, (256, 128, 8))
    return pl.pallas_call(
        _rms_body,
        grid=(m // tm,),
        in_specs=[pl.BlockSpec((tm, d), lambda i: (i, 0)), pl.BlockSpec((1, d), lambda i: (0, 0))],
        out_specs=pl.BlockSpec((tm, d), lambda i: (i, 0)),
        out_shape=jax.ShapeDtypeStruct((m, d), out_dtype),
        compiler_params=_cparams(1),
        name="rmsnorm",
    )(x, g.reshape(1, d))


def _rope_tables(pos, rot_dim, period):
    half = rot_dim // 2
    inv = ROPE_THETA ** (-jnp.arange(half, dtype=F32) * 2.0 / rot_dim)
    ang = pos.astype(F32)[:, None] * inv[None, :]
    cos, sin = jnp.cos(ang), jnp.sin(ang)
    t = pos.shape[0]
    z_half = jnp.zeros((t, half), F32)
    z_rest = jnp.zeros((t, period - rot_dim), F32)
    c = jnp.concatenate([cos, cos, jnp.ones((t, period - rot_dim), F32)], axis=1)
    sm = jnp.concatenate([-sin, z_half, z_rest], axis=1)
    sp = jnp.concatenate([z_half, sin, z_rest], axis=1)
    reps = LANES // period
    return tuple(jnp.tile(a, (1, reps)) for a in (c, sm, sp))


def _rope128(x, tabs, half):
    c, sm, sp = tabs
    return x * c + pltpu.roll(x, LANES - half, 1) * sm + pltpu.roll(x, half, 1) * sp


def _post_even_body(z_ref, qn_ref, kvn_ref, c1, s1m, s1p, c2, s2m, s2p,
                    cqn_ref, ckv_ref, kr_ref, mq_ref, mk_ref, mv_ref, *, offs, rope_dim):
    o_cq, o_ckv, o_mq, o_mk, o_mv, o_kr, _ = offs
    cqn_ref[...] = _rms(z_ref[:, o_cq:o_ckv], qn_ref[...]).astype(cqn_ref.dtype)
    ckv_ref[...] = _rms(z_ref[:, o_ckv:o_mq], kvn_ref[...])
    t1 = (c1[...], s1m[...], s1p[...])
    for g in range((o_mk - o_mq) // LANES):
        mq_ref[:, g * LANES:(g + 1) * LANES] = _rope128(z_ref[:, o_mq + g * LANES:o_mq + (g + 1) * LANES], t1, ROT_DIM // 2)
    for g in range((o_mv - o_mk) // LANES):
        mk_ref[:, g * LANES:(g + 1) * LANES] = _rope128(z_ref[:, o_mk + g * LANES:o_mk + (g + 1) * LANES], t1, ROT_DIM // 2)
    mv_ref[...] = z_ref[:, o_mv:o_kr]
    t2 = (c2[...], s2m[...], s2p[...])
    kr_ref[...] = _rope128(z_ref[:, o_kr:o_kr + LANES], t2, rope_dim // 2)[:, :rope_dim]


def _post_even(z, q_norm, kv_norm, tabs_head, tabs_mla, offs, rope_dim):
    m, nz = z.shape
    o_cq, o_ckv, o_mq, o_mk, o_mv, o_kr, _ = offs
    tm = _tile(m, (256, 128, 8))
    row = lambda w: pl.BlockSpec((tm, w), lambda i: (i, 0))
    full = lambda w: pl.BlockSpec((1, w), lambda i: (0, 0))
    widths = (o_ckv - o_cq, o_mq - o_ckv, rope_dim, o_mk - o_mq, o_mv - o_mk, o_kr - o_mv)
    dtypes = (BF16, F32, F32, F32, F32, F32)
    return pl.pallas_call(
        functools.partial(_post_even_body, offs=offs, rope_dim=rope_dim),
        grid=(m // tm,),
        in_specs=[row(nz), full(widths[0]), full(widths[1])] + [row(LANES)] * 6,
        out_specs=[row(w) for w in widths],
        out_shape=[jax.ShapeDtypeStruct((m, w), d) for w, d in zip(widths, dtypes)],
        compiler_params=_cparams(1),
        name="post_even",
    )(z, q_norm.reshape(1, -1), kv_norm.reshape(1, -1), *tabs_head, *tabs_mla)


def _post_q_body(q_ref, c2, s2m, s2p, qn_ref, qr_ref, *, n_nope, n_heads, rope_dim):
    qn_ref[...] = q_ref[:, :n_nope].astype(qn_ref.dtype)
    t2 = (c2[...], s2m[...], s2p[...])
    per = LANES // rope_dim
    for g in range(n_heads // per):
        r = _rope128(q_ref[:, n_nope + g * LANES:n_nope + (g + 1) * LANES], t2, rope_dim // 2)
        for u in range(per):
            qr_ref[g * per + u] = r[:, u * rope_dim:(u + 1) * rope_dim].astype(qr_ref.dtype)


def _post_q(q, tabs_mla, n_heads, rope_dim):
    m = q.shape[0]
    n_nope = n_heads * HEAD_DIM
    tm = _tile(m, (256, 128, 8))
    return pl.pallas_call(
        functools.partial(_post_q_body, n_nope=n_nope, n_heads=n_heads, rope_dim=rope_dim),
        grid=(m // tm,),
        in_specs=[pl.BlockSpec((tm, q.shape[1]), lambda i: (i, 0))] + [pl.BlockSpec((tm, LANES), lambda i: (i, 0))] * 3,
        out_specs=[pl.BlockSpec((tm, n_nope), lambda i: (i, 0)),
                   pl.BlockSpec((n_heads, tm, rope_dim), lambda i: (0, i, 0))],
        out_shape=[jax.ShapeDtypeStruct((m, n_nope), BF16), jax.ShapeDtypeStruct((n_heads, m, rope_dim), BF16)],
        compiler_params=_cparams(1),
        name="post_q",
    )(q, *tabs_mla)


def _post_odd_body(z_ref, c1, s1m, s1p, c3, s3m, s3p, q_ref, k_ref, v_ref, qi_ref, ki_ref, wi_ref,
                   *, offs, n_idx_heads, w_scale):
    o_q, o_k, o_v, o_qi, o_ki = offs
    t1 = (c1[...], s1m[...], s1p[...])
    t3 = (c3[...], s3m[...], s3p[...])
    for g in range((o_k - o_q) // LANES):
        q_ref[:, g * LANES:(g + 1) * LANES] = _rope128(
            z_ref[:, o_q + g * LANES:o_q + (g + 1) * LANES], t1, ROT_DIM // 2).astype(q_ref.dtype)
    k_ref[...] = _rope128(z_ref[:, o_k:o_v], t1, ROT_DIM // 2)
    v_ref[...] = z_ref[:, o_v:o_qi]
    idx_half = IDX_DIM // 4 // 2
    for g in range((o_ki - o_qi) // LANES):
        qi_ref[:, g * LANES:(g + 1) * LANES] = _rope128(
            z_ref[:, o_qi + g * LANES:o_qi + (g + 1) * LANES], t3, idx_half).astype(qi_ref.dtype)
    tail = z_ref[:, o_ki:o_ki + LANES]
    ki_ref[...] = _rope128(tail, t3, idx_half)[:, :IDX_DIM]
    wi_ref[...] = tail[:, IDX_DIM:IDX_DIM + n_idx_heads] * w_scale


def _post_odd(z, tabs_head, tabs_idx, offs, n_idx_heads):
    m, nz = z.shape
    o_q, o_k, o_v, o_qi, o_ki = offs
    tm = _tile(m, (256, 128, 8))
    row = lambda w: pl.BlockSpec((tm, w), lambda i: (i, 0))
    widths = (o_k - o_q, o_v - o_k, o_qi - o_v, o_ki - o_qi, IDX_DIM, n_idx_heads)
    dtypes = (BF16, F32, F32, BF16, F32, F32)
    w_scale = (n_idx_heads ** -0.5) * (IDX_DIM ** -0.5)
    return pl.pallas_call(
        functools.partial(_post_odd_body, offs=offs, n_idx_heads=n_idx_heads, w_scale=w_scale),
        grid=(m // tm,),
        in_specs=[row(nz)] + [row(LANES)] * 6,
        out_specs=[row(w) for w in widths],
        out_shape=[jax.ShapeDtypeStruct((m, w), d) for w, d in zip(widths, dtypes)],
        compiler_params=_cparams(1),
        name="post_odd",
    )(z, *tabs_head, *tabs_idx)


def _softmax_step(carry, s, v_bf):
    m, l, acc = carry
    m_new = jnp.maximum(m, jnp.max(s, axis=1, keepdims=True))
    a = jnp.exp(m - m_new)
    p = jnp.exp(s - m_new)
    l = a * l + jnp.sum(p, axis=1, keepdims=True)
    acc = a * acc + _dot(p.astype(BF16), v_bf)
    return m_new, l, acc


def _softmax_init(tq, dv):
    return (jnp.full((tq, 1), M_INIT, F32), jnp.zeros((tq, 1), F32), jnp.zeros((tq, dv), F32))


def _causal_keep(tq, tk, q0, k0):
    qpos = q0 + lax.broadcasted_iota(jnp.int32, (tq, tk), 0)
    kpos = k0 + lax.broadcasted_iota(jnp.int32, (tq, tk), 1)
    return kpos <= qpos


def _head_cols(u):
    return slice(u * HEAD_DIM, (u + 1) * HEAD_DIM)


def _write_heads(o_ref, carries):
    for u, (_, l, acc) in enumerate(carries):
        o_ref[:, _head_cols(u)] = (acc / l).astype(o_ref.dtype)


def _prompt_tiles(seq):
    return _tile(seq, (256, 128)), _tile(seq, (512, 256, 128))


def _mla_prompt_body(qn_ref, qr_ref, kn_ref, kr_ref, v_ref, o_ref, *, tq, tk, nh, scale):
    q0 = pl.program_id(2) * tq
    qn = [qn_ref[:, _head_cols(u)] for u in range(nh)]
    qr = [qr_ref[u] for u in range(nh)]

    def step(j, carry, masked):
        k0 = pl.multiple_of(j * tk, tk)
        krb = kr_ref[pl.ds(k0, tk), :].astype(BF16)
        keep = _causal_keep(tq, tk, q0, k0) if masked else None
        out = []
        for u in range(nh):
            s = (_dot_nt(qn[u], kn_ref[pl.ds(k0, tk), _head_cols(u)]) + _dot_nt(qr[u], krb)) * scale
            if masked:
                s = jnp.where(keep, s, NEG)
            out.append(_softmax_step(carry[u], s, v_ref[pl.ds(k0, tk), _head_cols(u)]))
        return tuple(out)

    n_full = q0 // tk
    carry = tuple(_softmax_init(tq, HEAD_DIM) for _ in range(nh))
    carry = lax.fori_loop(0, n_full, functools.partial(step, masked=False), carry)
    carry = lax.fori_loop(n_full, (q0 + tq + tk - 1) // tk, functools.partial(step, masked=True), carry)
    _write_heads(o_ref, carry)


def _mla_prompt(qn, qr, kn, kr, v, batch, seq, n_heads, scale):
    tq, tk = _prompt_tiles(seq)
    nq = seq // tq
    nh = _tile(n_heads, (MLA_HEADS_PER_STEP, 2, 1))
    rope_dim = kr.shape[1]
    return pl.pallas_call(
        functools.partial(_mla_prompt_body, tq=tq, tk=tk, nh=nh, scale=scale),
        grid=(batch, n_heads // nh, nq),
        in_specs=[pl.BlockSpec((tq, nh * HEAD_DIM), lambda b, h, i: (b * nq + i, h)),
                  pl.BlockSpec((nh, tq, rope_dim), lambda b, h, i: (h, b * nq + i, 0)),
                  pl.BlockSpec((seq, nh * HEAD_DIM), lambda b, h, i: (b, h)),
                  pl.BlockSpec((seq, rope_dim), lambda b, h, i: (b, 0)),
                  pl.BlockSpec((seq, nh * HEAD_DIM), lambda b, h, i: (b, h))],
        out_specs=pl.BlockSpec((tq, nh * HEAD_DIM), lambda b, h, i: (b * nq + i, h)),
        out_shape=jax.ShapeDtypeStruct(qn.shape, BF16),
        compiler_params=_cparams(3),
        name="mla_prompt",
    )(qn, qr, kn, kr, v)


def _topk_lane_mask(gate, n_valid, n_top):
    lane = lax.broadcasted_iota(jnp.int32, gate.shape, 1)
    rank = jnp.zeros(gate.shape, F32)
    for m in range(n_valid):
        col = gate[:, m:m + 1]
        beats = (col > gate) | ((col == gate) & (m < lane))
        rank = rank + jnp.where(beats, 1.0, 0.0)
    return rank < n_top


def _moba_prompt_body(q_ref, k_ref, v_ref, o_ref, *, tq, blk, nh, n_blocks, n_top, scale):
    q0 = pl.program_id(2) * tq
    own = q0 // blk
    w_lanes = _round_up(n_blocks, SUBLANES)
    lane = lax.broadcasted_iota(jnp.int32, (tq, w_lanes), 1)
    k_means = [jnp.mean(k_ref[n * blk:(n + 1) * blk, :], axis=0, keepdims=True) for n in range(n_blocks)]
    k_means = jnp.concatenate(k_means + [jnp.zeros((w_lanes - n_blocks, HEAD_DIM), F32)] * (w_lanes > n_blocks), axis=0)
    qb, sel_bias = [], []
    for u in range(nh):
        q = q_ref[:, _head_cols(u)]
        qb.append(q.astype(BF16))
        gate = lax.dot_general(q, k_means, NT, precision=lax.Precision.HIGHEST, preferred_element_type=F32)
        gate = jnp.where(lane < own, gate, -jnp.inf)
        sel = _topk_lane_mask(gate, n_blocks, n_top) & (lane < own)
        sel_bias.append(jnp.where(sel, 0.0, NEG))

    def step(n, carry, is_own):
        k0 = pl.multiple_of(n * blk, blk)
        kb = k_ref[pl.ds(k0, blk), :].astype(BF16)
        vb = v_ref[pl.ds(k0, blk), :].astype(BF16)
        keep = _causal_keep(tq, blk, q0, k0) if is_own else None
        out = []
        for u in range(nh):
            s = _dot_nt(qb[u], kb) * scale
            if is_own:
                s = jnp.where(keep, s, NEG)
            else:
                s = s + jnp.sum(jnp.where(lane == n, sel_bias[u], 0.0), axis=1, keepdims=True)
            out.append(_softmax_step(carry[u], s, vb))
        return tuple(out)

    carry = tuple(_softmax_init(tq, HEAD_DIM) for _ in range(nh))
    carry = lax.fori_loop(0, own, functools.partial(step, is_own=False), carry)
    _write_heads(o_ref, step(own, carry, is_own=True))


def _moba_prompt(q, k, v, batch, seq, n_heads, n_kv):
    blk = MOBA_BLOCK
    tq = min(blk, 256)
    nq = seq // tq
    n_blocks = seq // blk
    group = n_heads // n_kv
    nh = _tile(group, (HEADS_PER_STEP, 2, 1))
    steps_per_kv = group // nh
    return pl.pallas_call(
        functools.partial(_moba_prompt_body, tq=tq, blk=blk, nh=nh, n_blocks=n_blocks,
                          n_top=min(MOBA_TOPK, n_blocks), scale=HEAD_DIM ** -0.5),
        grid=(batch, n_heads // nh, nq),
        in_specs=[pl.BlockSpec((tq, nh * HEAD_DIM), lambda b, h, i: (b * nq + i, h)),
                  pl.BlockSpec((seq, HEAD_DIM), lambda b, h, i: (b, h // steps_per_kv)),
                  pl.BlockSpec((seq, HEAD_DIM), lambda b, h, i: (b, h // steps_per_kv))],
        out_specs=pl.BlockSpec((tq, nh * HEAD_DIM), lambda b, h, i: (b * nq + i, h)),
        out_shape=jax.ShapeDtypeStruct(q.shape, BF16),
        compiler_params=_cparams(3),
        name="moba_prompt",
    )(q, k, v)


def _sort_key(x):
    b = lax.bitcast_convert_type(x, jnp.int32)
    return jnp.where(b < 0, b ^ jnp.int32(0x7FFFFFFF), b)


def _kth_key(count_ge, rows, k):
    kf = jnp.float32(k)
    int_min = jnp.int32(-2 ** 31)
    t0 = jnp.where(count_ge(jnp.zeros((rows, 1), jnp.int32)) >= kf, jnp.int32(0), int_min)

    def step(it, t):
        cand = t | lax.shift_left(jnp.int32(1), jnp.int32(30) - it)
        return jnp.where(count_ge(cand) >= kf, cand, t)

    return lax.fori_loop(0, 31, step, jnp.broadcast_to(t0, (rows, 1)))


def _dsa_select_prompt_body(qi_ref, wi_ref, ki_ref, bias_ref, key_scr, *, tq, tk, n_chunks, n_idx_heads, n_keep):
    q0 = pl.program_id(1) * tq
    wi = wi_ref[...]
    for c in range(n_chunks):
        @pl.when(c * tk <= q0 + tq - 1)
        def _():
            kc = ki_ref[c * tk:(c + 1) * tk, :].astype(BF16)
            sc = jnp.zeros((tq, tk), F32)
            for h in range(n_idx_heads):
                logits = _dot_nt(qi_ref[:, h * IDX_DIM:(h + 1) * IDX_DIM], kc)
                sc = sc + wi[:, h:h + 1] * jnp.maximum(logits, 0.0)
            qpos = q0 + lax.broadcasted_iota(jnp.int32, (tq, tk), 0)
            kpos = c * tk + lax.broadcasted_iota(jnp.int32, (tq, tk), 1)
            key_scr[c] = _sort_key(jnp.where(kpos <= qpos, sc, -jnp.inf))

    n_live = (q0 + tq - 1) // tk + 1

    def count_ge(t):
        def add(c, a):
            return a + jnp.sum(jnp.where(key_scr[c] >= t, 1.0, 0.0), axis=1, keepdims=True)
        return lax.fori_loop(0, n_live, add, jnp.zeros((tq, 1), F32))

    thr = _kth_key(count_ge, tq, n_keep)
    for c in range(n_chunks):
        @pl.when(c * tk <= q0 + tq - 1)
        def _():
            qpos = q0 + lax.broadcasted_iota(jnp.int32, (tq, tk), 0)
            kpos = c * tk + lax.broadcasted_iota(jnp.int32, (tq, tk), 1)
            keep = (key_scr[c] >= thr) & (kpos <= qpos)
            bias_ref[0, c] = jnp.where(keep, 0.0, NEG).astype(bias_ref.dtype)

        @pl.when(c * tk > q0 + tq - 1)
        def _():
            bias_ref[0, c] = jnp.full((tq, tk), NEG, bias_ref.dtype)


def _dsa_select_prompt(qi, wi, ki, batch, seq, n_keep):
    tq, tk = _prompt_tiles(seq)
    nq = seq // tq
    n_chunks = seq // tk
    n_idx_heads = wi.shape[1]
    return pl.pallas_call(
        functools.partial(_dsa_select_prompt_body, tq=tq, tk=tk, n_chunks=n_chunks,
                          n_idx_heads=n_idx_heads, n_keep=n_keep),
        grid=(batch, nq),
        in_specs=[pl.BlockSpec((tq, qi.shape[1]), lambda b, i: (b * nq + i, 0)),
                  pl.BlockSpec((tq, n_idx_heads), lambda b, i: (b * nq + i, 0)),
                  pl.BlockSpec((seq, IDX_DIM), lambda b, i: (b, 0))],
        out_specs=pl.BlockSpec((1, n_chunks, tq, tk), lambda b, i: (b, 0, i, 0)),
        out_shape=jax.ShapeDtypeStruct((batch, n_chunks, seq, tk), BF16),
        scratch_shapes=[pltpu.VMEM((n_chunks, tq, tk), jnp.int32)],
        compiler_params=_cparams(2),
        name="dsa_select_prompt",
    )(qi, wi, ki)


def _dsa_prompt_body(q_ref, k_ref, v_ref, bias_ref, o_ref, *, tq, tk, nh, scale):
    q0 = pl.program_id(1) * tq
    q = [q_ref[:, _head_cols(u)] for u in range(nh)]

    def step(j, carry):
        k0 = pl.multiple_of(j * tk, tk)
        kb = k_ref[pl.ds(k0, tk), :].astype(BF16)
        vb = v_ref[pl.ds(k0, tk), :].astype(BF16)
        bias = bias_ref[0, j].astype(F32)
        return tuple(_softmax_step(carry[u], _dot_nt(q[u], kb) * scale + bias, vb) for u in range(nh))

    init = tuple(_softmax_init(tq, HEAD_DIM) for _ in range(nh))
    _write_heads(o_ref, lax.fori_loop(0, (q0 + tq - 1) // tk + 1, step, init))


def _dsa_prompt(q, k, v, bias, batch, seq, n_heads):
    n_chunks, tk = bias.shape[1], bias.shape[3]
    tq = _prompt_tiles(seq)[0]
    nq = seq // tq
    nh = _tile(n_heads, (HEADS_PER_STEP, 2, 1))
    return pl.pallas_call(
        functools.partial(_dsa_prompt_body, tq=tq, tk=tk, nh=nh, scale=HEAD_DIM ** -0.5),
        grid=(batch, nq, n_heads // nh),
        in_specs=[pl.BlockSpec((tq, nh * HEAD_DIM), lambda b, i, h: (b * nq + i, h)),
                  pl.BlockSpec((seq, HEAD_DIM), lambda b, i, h: (b, 0)),
                  pl.BlockSpec((seq, HEAD_DIM), lambda b, i, h: (b, 0)),
                  pl.BlockSpec((1, n_chunks, tq, tk), lambda b, i, h: (b, 0, i, 0))],
        out_specs=pl.BlockSpec((tq, nh * HEAD_DIM), lambda b, i, h: (b * nq + i, h)),
        out_shape=jax.ShapeDtypeStruct(q.shape, BF16),
        compiler_params=_cparams(3),
        name="dsa_prompt",
    )(q, k, v, bias)


def _page_specs(block_tail, layer, n_per_step):
    zeros = (0,) * len(block_tail)

    def spec(r):
        return pl.BlockSpec((1, 1) + block_tail, lambda s, g, pt: (layer, pt[s, g * n_per_step + r]) + zeros)

    return [spec(r) for r in range(n_per_step)]


def _mla_decode_body(pt_ref, ql_ref, qr_ref, cn_ref, krn_ref, *rest, n_pages_step, scale):
    lat_refs = rest[:n_pages_step]
    kr_refs = rest[n_pages_step:2 * n_pages_step]
    o_ref, m_scr, l_scr, acc_scr = rest[2 * n_pages_step:]
    g = pl.program_id(1)
    ql = ql_ref[0]
    qr = qr_ref[0]

    @pl.when(g == 0)
    def _():
        s_self = (jnp.sum(ql * cn_ref[0], axis=1, keepdims=True)
                  + jnp.sum(qr * krn_ref[0], axis=1, keepdims=True)) * scale
        m_scr[...] = s_self
        l_scr[...] = jnp.ones_like(s_self)
        acc_scr[...] = jnp.broadcast_to(cn_ref[0], acc_scr.shape)

    lat = jnp.concatenate([r[0, 0].astype(BF16) for r in lat_refs], axis=0)
    kr_t = jnp.concatenate([r[0, 0].astype(BF16) for r in kr_refs], axis=1)
    s = (_dot_nt(ql.astype(BF16), lat) + _dot(qr.astype(BF16), kr_t)) * scale
    m = m_scr[...]
    m_new = jnp.maximum(m, jnp.max(s, axis=1, keepdims=True))
    a = jnp.exp(m - m_new)
    p = jnp.exp(s - m_new)
    m_scr[...] = m_new
    l_scr[...] = a * l_scr[...] + jnp.sum(p, axis=1, keepdims=True)
    acc_scr[...] = a * acc_scr[...] + _dot(p.astype(BF16), lat)

    @pl.when(g == pl.num_programs(1) - 1)
    def _():
        o_ref[0] = acc_scr[...] / l_scr[...]


def _mla_decode(q_lat, q_rope, c_new, kr_new, cache_lat, cache_kr_t, page_table, layer, scale):
    n_seq, n_heads, c_dim = q_lat.shape
    r_dim = q_rope.shape[2]
    page = cache_lat.shape[2]
    n_pages = page_table.shape[1]
    gp = _tile(n_pages, (PAGES_PER_STEP, 4, 2, 1))
    per_seq = lambda shape: pl.BlockSpec((1,) + shape, lambda s, g, pt: (s, 0, 0))
    grid_spec = pltpu.PrefetchScalarGridSpec(
        num_scalar_prefetch=1,
        grid=(n_seq, n_pages // gp),
        in_specs=[per_seq((n_heads, c_dim)), per_seq((n_heads, r_dim)), per_seq((1, c_dim)), per_seq((1, r_dim))]
        + _page_specs((page, c_dim), layer, gp) + _page_specs((r_dim, page), layer, gp),
        out_specs=per_seq((n_heads, c_dim)),
        scratch_shapes=[pltpu.VMEM((n_heads, 1), F32), pltpu.VMEM((n_heads, 1), F32),
                        pltpu.VMEM((n_heads, c_dim), F32)],
    )
    return pl.pallas_call(
        functools.partial(_mla_decode_body, n_pages_step=gp, scale=scale),
        grid_spec=grid_spec,
        out_shape=jax.ShapeDtypeStruct(q_lat.shape, F32),
        compiler_params=_cparams(2),
        name="mla_decode",
    )(page_table, q_lat, q_rope, c_new, kr_new, *([cache_lat] * gp), *([cache_kr_t] * gp))


def _moba_decode_body(pt_ref, q_ref, kn_ref, vn_ref, *rest, n_pages_step, pages_per_block, n_kv, n_top, scale):
    k_refs = rest[:n_pages_step]
    v_refs = rest[n_pages_step:2 * n_pages_step]
    o_ref, gate_scr, m_scr, l_scr, oblk_scr = rest[2 * n_pages_step:]
    g = pl.program_id(1)
    n_groups = pl.num_programs(1)
    q = q_ref[0]
    qb = q.astype(BF16)
    n_heads = q.shape[0]
    group = n_heads // n_kv
    head_kv = lax.broadcasted_iota(jnp.int32, (n_heads, 1), 0) // group
    lane = lax.broadcasted_iota(jnp.int32, gate_scr.shape, 1)
    blocks_step = n_pages_step // pages_per_block

    @pl.when(g == 0)
    def _():
        for scr in (gate_scr, m_scr, l_scr):
            scr[...] = jnp.zeros(scr.shape, F32)

    def per_kv(vals):
        out = vals[0]
        for kh in range(1, n_kv):
            out = jnp.where(head_kv == kh, vals[kh], out)
        return out

    rows_blk = pages_per_block * k_refs[0].shape[2]
    row_kv = lax.broadcasted_iota(jnp.int32, (n_heads, rows_blk), 1) & (n_kv - 1)
    own_rows = row_kv == head_kv
    sub_kv = lax.broadcasted_iota(jnp.int32, (SUBLANES, HEAD_DIM), 0) & (n_kv - 1)
    for jb in range(blocks_step):
        n = g * blocks_step + jb
        refs = range(jb * pages_per_block, (jb + 1) * pages_per_block)
        k_blk = jnp.concatenate([k_refs[r][0, 0] for r in refs], axis=0)
        v_blk = jnp.concatenate([v_refs[r][0, 0] for r in refs], axis=0).astype(BF16)
        k_sum = jnp.sum(k_blk.reshape(rows_blk // SUBLANES, SUBLANES, HEAD_DIM), axis=0)
        gates = []
        for kh in range(n_kv):
            k_mean = jnp.sum(jnp.where(sub_kv == kh, k_sum, 0.0), axis=0, keepdims=True) * (n_kv / rows_blk)
            gates.append(jnp.sum(q * k_mean, axis=1, keepdims=True))
        s = jnp.where(own_rows, _dot_nt(qb, k_blk.astype(BF16)) * scale, NEG)
        m_n = jnp.max(s, axis=1, keepdims=True)
        p = jnp.exp(s - m_n)
        gate_scr[...] = jnp.where(lane == n, per_kv(gates), gate_scr[...])
        m_scr[...] = jnp.where(lane == n, m_n, m_scr[...])
        l_scr[...] = jnp.where(lane == n, jnp.sum(p, axis=1, keepdims=True), l_scr[...])
        oblk_scr[n] = _dot(p.astype(BF16), v_blk)

    @pl.when(g == n_groups - 1)
    def _():
        n_past = n_groups * blocks_step
        in_past = lane < n_past
        gate = jnp.where(in_past, gate_scr[...], -jnp.inf)
        sel = _topk_lane_mask(gate, n_past, n_top) & in_past
        k_self = per_kv([kn_ref[0, :, kh, :] for kh in range(n_kv)])
        v_self = per_kv([vn_ref[0, :, kh, :] for kh in range(n_kv)])
        s_self = jnp.sum(q * k_self, axis=1, keepdims=True) * scale
        m_blk = jnp.where(sel, m_scr[...], NEG)
        m_tot = jnp.maximum(jnp.max(m_blk, axis=1, keepdims=True), s_self)
        w_blk = jnp.where(sel, jnp.exp(m_blk - m_tot), 0.0)
        w_self = jnp.exp(s_self - m_tot)
        l_tot = jnp.sum(w_blk * l_scr[...], axis=1, keepdims=True) + w_self
        o = w_self * v_self
        for nb in range(n_past):
            o = o + w_blk[:, nb:nb + 1] * oblk_scr[nb]
        o_ref[0] = o / l_tot


def _moba_decode(q, k_new, v_new, cache_k, cache_v, page_table, layer):
    n_seq, n_heads, _ = q.shape
    n_layers, pool, page, n_kv, _ = cache_k.shape
    n_pages = page_table.shape[1]
    ppb = MOBA_BLOCK // page
    gp = _tile(n_pages, (PAGES_PER_STEP, 4, 2, 1))
    assert MOBA_BLOCK % page == 0 and gp % ppb == 0 and (n_pages * page) % MOBA_BLOCK == 0
    assert n_kv & (n_kv - 1) == 0 and SUBLANES % n_kv == 0
    n_past = n_pages // ppb
    assert n_past <= LANES
    cache_k = cache_k.reshape(n_layers, pool, page * n_kv, HEAD_DIM)
    cache_v = cache_v.reshape(n_layers, pool, page * n_kv, HEAD_DIM)
    per_seq = lambda shape: pl.BlockSpec((1,) + shape, lambda s, g, pt: (s,) + (0,) * len(shape))
    grid_spec = pltpu.PrefetchScalarGridSpec(
        num_scalar_prefetch=1,
        grid=(n_seq, n_pages // gp),
        in_specs=[per_seq((n_heads, HEAD_DIM)), per_seq((1, n_kv, HEAD_DIM)), per_seq((1, n_kv, HEAD_DIM))]
        + _page_specs((page * n_kv, HEAD_DIM), layer, gp) + _page_specs((page * n_kv, HEAD_DIM), layer, gp),
        out_specs=per_seq((n_heads, HEAD_DIM)),
        scratch_shapes=[pltpu.VMEM((n_heads, LANES), F32)] * 3 + [pltpu.VMEM((n_past, n_heads, HEAD_DIM), F32)],
    )
    return pl.pallas_call(
        functools.partial(_moba_decode_body, n_pages_step=gp, pages_per_block=ppb, n_kv=n_kv,
                          n_top=min(MOBA_TOPK, n_past + 1), scale=HEAD_DIM ** -0.5),
        grid_spec=grid_spec,
        out_shape=jax.ShapeDtypeStruct(q.shape, F32),
        compiler_params=_cparams(2),
        name="moba_decode",
    )(page_table, q, k_new, v_new, *([cache_k] * gp), *([cache_v] * gp))


def _dsa_score_decode_body(pt_ref, qi_ref, wi_ref, kin_ref, *rest, n_pages_step):
    ki_refs = rest[:n_pages_step]
    o_ref = rest[n_pages_step]
    g = pl.program_id(1)
    qi = qi_ref[0]
    wi = wi_ref[0]

    @pl.when(g < pl.num_programs(1) - 1)
    def _():
        qb = qi.astype(BF16)
        logits = _dot(qb, jnp.concatenate([r[0, 0].astype(BF16) for r in ki_refs], axis=1))
        o_ref[0] = jnp.sum(wi * jnp.maximum(logits, 0.0), axis=0, keepdims=True)

    @pl.when(g == pl.num_programs(1) - 1)
    def _():
        logit = jnp.sum(qi * kin_ref[0], axis=1, keepdims=True)
        sc = jnp.sum(wi * jnp.maximum(logit, 0.0), axis=0, keepdims=True)
        lane = lax.broadcasted_iota(jnp.int32, o_ref.shape[1:], 1)
        o_ref[0] = jnp.where(lane == 0, sc, -jnp.inf)


def _dsa_score_decode(qi, wi, ki_new, cache_ki_t, page_table, layer):
    n_seq, n_ih, _ = qi.shape
    page = cache_ki_t.shape[3]
    n_pages = page_table.shape[1]
    gp = _tile(n_pages, (SCORE_PAGES_PER_STEP, PAGES_PER_STEP, 4, 2, 1))
    n_groups = n_pages // gp
    per_seq = lambda shape: pl.BlockSpec((1,) + shape, lambda s, g, pt: (s, 0, 0))
    zeros = (0, 0)

    def page_spec(r):
        return pl.BlockSpec((1, 1, IDX_DIM, page),
                            lambda s, g, pt: (layer, pt[s, jnp.minimum(g, n_groups - 1) * gp + r]) + zeros)

    grid_spec = pltpu.PrefetchScalarGridSpec(
        num_scalar_prefetch=1,
        grid=(n_seq, n_groups + 1),
        in_specs=[per_seq((n_ih, IDX_DIM)), per_seq((n_ih, 1)), per_seq((1, IDX_DIM))]
        + [page_spec(r) for r in range(gp)],
        out_specs=pl.BlockSpec((1, 1, gp * page), lambda s, g, pt: (s, 0, g)),
    )
    return pl.pallas_call(
        functools.partial(_dsa_score_decode_body, n_pages_step=gp),
        grid_spec=grid_spec,
        out_shape=jax.ShapeDtypeStruct((n_seq, 1, (n_groups + 1) * gp * page), F32),
        compiler_params=_cparams(2),
        name="dsa_score_decode",
    )(page_table, qi, wi, ki_new, *([cache_ki_t] * gp))


def _dsa_select_decode_body(sc_ref, bias_ref, *, n_keys, n_keep):
    key = _sort_key(sc_ref[...])
    rows = key.shape[0]
    thr = _kth_key(lambda t: jnp.sum(jnp.where(key >= t, 1.0, 0.0), axis=1, keepdims=True), rows, n_keep)
    lane = lax.broadcasted_iota(jnp.int32, key.shape, 1)
    bias_ref[...] = jnp.where((key >= thr) & (lane < n_keys), 0.0, NEG)


def _dsa_select_decode(scores, n_keys, n_keep):
    return pl.pallas_call(
        functools.partial(_dsa_select_decode_body, n_keys=n_keys, n_keep=n_keep),
        out_shape=jax.ShapeDtypeStruct(scores.shape, F32),
        compiler_params=pltpu.CompilerParams(vmem_limit_bytes=VMEM_LIMIT),
        name="dsa_select_decode",
    )(scores)


def _dsa_decode_body(pt_ref, q_ref, kn_ref, vn_ref, bias_ref, bself_ref, *rest, n_pages_step, scale):
    k_refs = rest[:n_pages_step]
    v_refs = rest[n_pages_step:2 * n_pages_step]
    o_ref, m_scr, l_scr, acc_scr = rest[2 * n_pages_step:]
    g = pl.program_id(1)
    q = q_ref[0]

    @pl.when(g == 0)
    def _():
        s_self = jnp.sum(q * kn_ref[0], axis=1, keepdims=True) * scale + bself_ref[0][:, 0:1]
        m_scr[...] = jnp.maximum(s_self, M_INIT)
        p_self = jnp.exp(s_self - m_scr[...])
        l_scr[...] = p_self
        acc_scr[...] = p_self * vn_ref[0]

    k = jnp.concatenate([r[0, 0].astype(BF16) for r in k_refs], axis=0)
    v = jnp.concatenate([r[0, 0].astype(BF16) for r in v_refs], axis=0)
    s = _dot_nt(q.astype(BF16), k) * scale + bias_ref[0]
    m = m_scr[...]
    m_new = jnp.maximum(m, jnp.max(s, axis=1, keepdims=True))
    a = jnp.exp(m - m_new)
    p = jnp.exp(s - m_new)
    m_scr[...] = m_new
    l_scr[...] = a * l_scr[...] + jnp.sum(p, axis=1, keepdims=True)
    acc_scr[...] = a * acc_scr[...] + _dot(p.astype(BF16), v)

    @pl.when(g == pl.num_programs(1) - 1)
    def _():
        o_ref[0] = acc_scr[...] / l_scr[...]


def _dsa_decode(q, k_new, v_new, bias, cache_k, cache_v, page_table, layer):
    n_seq, n_heads, _ = q.shape
    page = cache_k.shape[2]
    n_pages = page_table.shape[1]
    gp = _tile(n_pages, (PAGES_PER_STEP, 4, 2, 1))
    n_groups = n_pages // gp
    per_seq = lambda shape: pl.BlockSpec((1,) + shape, lambda s, g, pt: (s, 0, 0))
    grid_spec = pltpu.PrefetchScalarGridSpec(
        num_scalar_prefetch=1,
        grid=(n_seq, n_groups),
        in_specs=[per_seq((n_heads, HEAD_DIM)), per_seq((1, HEAD_DIM)), per_seq((1, HEAD_DIM)),
                  pl.BlockSpec((1, 1, gp * page), lambda s, g, pt: (s, 0, g)),
                  pl.BlockSpec((1, 1, gp * page), lambda s, g, pt: (s, 0, n_groups))]
        + _page_specs((page, HEAD_DIM), layer, gp) + _page_specs((page, HEAD_DIM), layer, gp),
        out_specs=per_seq((n_heads, HEAD_DIM)),
        scratch_shapes=[pltpu.VMEM((n_heads, 1), F32), pltpu.VMEM((n_heads, 1), F32),
                        pltpu.VMEM((n_heads, HEAD_DIM), F32)],
    )
    return pl.pallas_call(
        functools.partial(_dsa_decode_body, n_pages_step=gp, scale=HEAD_DIM ** -0.5),
        grid_spec=grid_spec,
        out_shape=jax.ShapeDtypeStruct(q.shape, F32),
        compiler_params=_cparams(2),
        name="dsa_decode",
    )(page_table, q, k_new, v_new, bias, bias, *([cache_k] * gp), *([cache_v] * gp))


def _silu_gate(u_conv, g):
    return u_conv / (1.0 + jnp.exp(-u_conv)) * g


def _ffn_up_prompt_body(x_ref, wu_ref, wg_ref, cw_ref, cb_ref, act_ref, tail_ref, u_scr, *, tm, tiles_per_seq):
    i = pl.program_id(1)

    @pl.when(i % tiles_per_seq == 0)
    def _():
        u_scr[0:SUBLANES, :] = jnp.zeros((SUBLANES, u_scr.shape[1]), F32)

    x = x_ref[...]
    u = _dot(x, wu_ref[...])
    g = _dot(x, wg_ref[...])
    u_scr[SUBLANES:SUBLANES + tm, :] = u
    u_conv = (cb_ref[...] + cw_ref[0:1, :] * u_scr[SUBLANES - 2:SUBLANES - 2 + tm, :]
              + cw_ref[1:2, :] * u_scr[SUBLANES - 1:SUBLANES - 1 + tm, :] + cw_ref[2:3, :] * u)
    act_ref[...] = _silu_gate(u_conv, g).astype(act_ref.dtype)
    tail = u_scr[tm:tm + SUBLANES, :]
    u_scr[0:SUBLANES, :] = tail
    tail_ref[0] = tail


def _ffn_up_prompt(x, w_in, conv_w, conv_b, batch, seq):
    m, d = x.shape
    f = w_in.shape[1] // 2
    tm = _tile(seq, (1024, 512, 256, 128))
    tf = _tile(f, (512, 256, 128))
    nf = f // tf
    tps = seq // tm
    return pl.pallas_call(
        functools.partial(_ffn_up_prompt_body, tm=tm, tiles_per_seq=tps),
        grid=(nf, m // tm),
        in_specs=[pl.BlockSpec((tm, d), lambda j, i: (i, 0)),
                  pl.BlockSpec((d, tf), lambda j, i: (0, j)),
                  pl.BlockSpec((d, tf), lambda j, i: (0, j + nf)),
                  pl.BlockSpec((conv_w.shape[0], tf), lambda j, i: (0, j)),
                  pl.BlockSpec((1, tf), lambda j, i: (0, j))],
        out_specs=[pl.BlockSpec((tm, tf), lambda j, i: (i, j)),
                   pl.BlockSpec((1, SUBLANES, tf), lambda j, i: (i // tps, 0, j))],
        out_shape=[jax.ShapeDtypeStruct((m, f), BF16), jax.ShapeDtypeStruct((batch, SUBLANES, f), F32)],
        scratch_shapes=[pltpu.VMEM((tm + SUBLANES, tf), F32)],
        compiler_params=_cparams(2),
        name="ffn_up_prompt",
    )(x, w_in, w_in, conv_w, conv_b.reshape(1, f))


def _ffn_up_decode_body(x_ref, wu_ref, wg_ref, cw_ref, cb_ref, p2_ref, p1_ref, act_ref, u_ref):
    x = x_ref[...]
    u = _dot(x, wu_ref[...])
    g = _dot(x, wg_ref[...])
    u_conv = cb_ref[...] + cw_ref[0:1, :] * p2_ref[...] + cw_ref[1:2, :] * p1_ref[...] + cw_ref[2:3, :] * u
    act_ref[...] = _silu_gate(u_conv, g).astype(act_ref.dtype)
    u_ref[...] = u


def _ffn_up_decode(x, w_in, conv_w, conv_b, prev2, prev1):
    m, d = x.shape
    f = w_in.shape[1] // 2
    tf = _tile(f, (512, 256, 128))
    nf = f // tf
    col = lambda rows: pl.BlockSpec((rows, tf), lambda j: (0, j))
    return pl.pallas_call(
        _ffn_up_decode_body,
        grid=(nf,),
        in_specs=[pl.BlockSpec((m, d), lambda j: (0, 0)), pl.BlockSpec((d, tf), lambda j: (0, j)),
                  pl.BlockSpec((d, tf), lambda j: (0, j + nf)), col(conv_w.shape[0]), col(1), col(m), col(m)],
        out_specs=[col(m), col(m)],
        out_shape=[jax.ShapeDtypeStruct((m, f), BF16), jax.ShapeDtypeStruct((m, f), F32)],
        compiler_params=_cparams(1),
        name="ffn_up_decode",
    )(x, w_in, w_in, conv_w, conv_b.reshape(1, f), prev2, prev1)


def _pad_cols(w, n):
    return jnp.pad(w, ((0, 0), (0, n - w.shape[1])))


def _even_weights(w_in, w_q_up, w_uk, w_uv, w_out, q_lora, kv_lora, rope_dim, n_moba, n_kv):
    sizes = (q_lora, kv_lora, rope_dim, n_moba * HEAD_DIM, n_kv * HEAD_DIM, n_kv * HEAD_DIM)
    cuts = [0]
    for s in sizes:
        cuts.append(cuts[-1] + s)
    cq, ckv, kr, mq, mk, mv = [w_in[:, a:b] for a, b in zip(cuts[:-1], cuts[1:])]
    order = (cq, ckv, mq, mk, mv, kr)
    offs = [0]
    for p in order:
        offs.append(offs[-1] + p.shape[1])
    n_pad = _round_up(offs[-2] + LANES, 2 * LANES)
    w_in_r = _pad_cols(jnp.concatenate(order, axis=1), n_pad).astype(BF16)
    n_heads = w_uk.shape[1]
    nope = w_uk.shape[2]
    wq = w_q_up.reshape(q_lora, n_heads, nope + rope_dim)
    wq_r = jnp.concatenate([wq[:, :, :nope].reshape(q_lora, -1), wq[:, :, nope:].reshape(q_lora, -1)], axis=1).astype(BF16)
    return dict(w_in=w_in_r, offs=tuple(offs), w_q=wq_r,
                w_uk_flat=w_uk.reshape(kv_lora, -1).astype(BF16), w_uv_flat=w_uv.reshape(kv_lora, -1).astype(BF16),
                w_uk_t=jnp.transpose(w_uk, (1, 2, 0)).astype(BF16), w_uv_h=jnp.transpose(w_uv, (1, 0, 2)).astype(BF16),
                w_out=w_out.astype(BF16))


def _even_project(hn, ew, q_norm, kv_norm, tabs_head, tabs_mla, rope_dim, n_mla):
    z = _mm(hn, ew["w_in"])
    cqn, ckv, kr, mq, mk, mv = _post_even(z, q_norm, kv_norm, tabs_head, tabs_mla, ew["offs"], rope_dim)
    qn, qr = _post_q(_mm(cqn, ew["w_q"]), tabs_mla, n_mla, rope_dim)
    return qn, qr, ckv, kr, mq, mk, mv


def _ffn(h, hn, w_in, conv_w, conv_b, w_out, prompt_dims=None, state=None):
    if prompt_dims is not None:
        act, tail = _ffn_up_prompt(hn, w_in, conv_w, conv_b, *prompt_dims)
        new_state = tail[:, SUBLANES - (conv_w.shape[0] - 1):, :]
    else:
        act, u = _ffn_up_decode(hn, w_in, conv_w, conv_b, state[:, 0, :], state[:, 1, :])
        new_state = jnp.stack([state[:, 1, :], u], axis=1)
    return _mm(act, w_out, res=h), new_state


def kernel(x_prompt, x_sample, cache_mla_latent, cache_mla_krope, cache_moba_k, cache_moba_v, cache_dsa_k, cache_dsa_v, cache_dsa_kidx, state_ffn_conv, page_table, norm_attn, norm_ffn, norm_final, w_in_ab, mla_q_norm, mla_w_q_up, mla_kv_norm, mla_w_uk, mla_w_uv, w_out_ab, w_in_c, w_out_c, w_ffn_in, conv_w, conv_b, w_ffn_out):
    batch, seq, d_model = x_prompt.shape
    n_dec, dec_seq, _ = x_sample.shape
    depth = norm_attn.shape[0]
    assert dec_seq == 1 and conv_w.shape[1] == 3 and seq & (seq - 1) == 0
    page = cache_mla_latent.shape[2]
    past_len = page_table.shape[1] * page
    q_lora, kv_lora, rope_dim = mla_q_norm.shape[1], mla_kv_norm.shape[1], cache_mla_krope.shape[3]
    n_mla, nope = mla_w_uk.shape[2], mla_w_uk.shape[3]
    n_kv = cache_moba_k.shape[3]
    n_moba = (w_in_ab.shape[2] - q_lora - kv_lora - rope_dim) // HEAD_DIM - 2 * n_kv
    n_dsa = w_out_c.shape[1] // HEAD_DIM
    n_idx = (w_in_c.shape[2] - (n_dsa + 2) * HEAD_DIM - IDX_DIM) // (IDX_DIM + 1)
    assert nope == HEAD_DIM and mla_w_uv.shape[3] == HEAD_DIM and cache_dsa_kidx.shape[3] == IDX_DIM
    tp = batch * seq
    mla_scale = 1.0 / math.sqrt(nope + rope_dim)

    pos_p = jnp.tile(jnp.arange(seq, dtype=jnp.int32), batch)
    pos_s = jnp.full((n_dec,), past_len, jnp.int32)
    tabs = {}
    for name, pos in (("p", pos_p), ("s", pos_s)):
        tabs[name] = dict(head=_rope_tables(pos, ROT_DIM, HEAD_DIM), mla=_rope_tables(pos, rope_dim, rope_dim),
                          idx=_rope_tables(pos, IDX_DIM // 4, IDX_DIM))

    cache_kr_t = jnp.swapaxes(cache_mla_krope, 2, 3)
    cache_ki_t = jnp.swapaxes(cache_dsa_kidx, 2, 3)

    h_p = x_prompt.reshape(tp, d_model)
    h_s = x_sample.reshape(n_dec, d_model)
    outs = {k: [] for k in ("lat_p", "lat_s", "kr_p", "kr_s", "mk_p", "mk_s", "mv_p", "mv_s",
                            "dk_p", "dk_s", "dv_p", "dv_s", "dki_p", "dki_s", "cv_p", "cv_s")}

    for layer in range(depth):
        i = layer // 2
        hn_p = _rmsnorm(h_p, norm_attn[layer], BF16)
        hn_s = _rmsnorm(h_s, norm_attn[layer], BF16)
        if layer % 2 == 0:
            ew = _even_weights(w_in_ab[i], mla_w_q_up[i], mla_w_uk[i], mla_w_uv[i], w_out_ab[i],
                               q_lora, kv_lora, rope_dim, n_moba, n_kv)
            qn, qr, ckv, kr, mq, mk, mv = _even_project(hn_p, ew, mla_q_norm[i], mla_kv_norm[i],
                                                        tabs["p"]["head"], tabs["p"]["mla"], rope_dim, n_mla)
            ckv_b = ckv.astype(BF16)
            o_mla = _mla_prompt(qn, qr, _mm(ckv_b, ew["w_uk_flat"], out_dtype=BF16), kr,
                                _mm(ckv_b, ew["w_uv_flat"], out_dtype=BF16), batch, seq, n_mla, mla_scale)
            o_moba = _moba_prompt(mq, mk, mv, batch, seq, n_moba, n_kv)
            h_p = _mm(jnp.concatenate([o_mla, o_moba], axis=1), ew["w_out"], res=h_p)
            outs["lat_p"].append(ckv.reshape(batch, seq, kv_lora))
            outs["kr_p"].append(kr.reshape(batch, seq, rope_dim))
            outs["mk_p"].append(mk.reshape(batch, seq, n_kv, HEAD_DIM))
            outs["mv_p"].append(mv.reshape(batch, seq, n_kv, HEAD_DIM))
            qn, qr, ckv, kr, mq, mk, mv = _even_project(hn_s, ew, mla_q_norm[i], mla_kv_norm[i],
                                                        tabs["s"]["head"], tabs["s"]["mla"], rope_dim, n_mla)
            q_lat = _bmm_heads(qn, ew["w_uk_t"]).reshape(n_dec, n_mla, kv_lora)
            o_lat = _mla_decode(q_lat, jnp.transpose(qr, (1, 0, 2)).astype(F32), ckv.reshape(n_dec, 1, kv_lora),
                                kr.reshape(n_dec, 1, rope_dim), cache_mla_latent, cache_kr_t,
                                page_table, i, mla_scale)
            o_mla = _bmm_heads(o_lat.reshape(n_dec, n_mla * kv_lora), ew["w_uv_h"])
            o_moba = _moba_decode(mq.reshape(n_dec, n_moba, HEAD_DIM), mk.reshape(n_dec, 1, n_kv, HEAD_DIM),
                                  mv.reshape(n_dec, 1, n_kv, HEAD_DIM), cache_moba_k, cache_moba_v, page_table, i)
            o_cat = jnp.concatenate([o_mla, o_moba.reshape(n_dec, n_moba * HEAD_DIM)], axis=1).astype(BF16)
            h_s = _mm(o_cat, ew["w_out"], res=h_s)
            outs["lat_s"].append(ckv.reshape(n_dec, 1, kv_lora))
            outs["kr_s"].append(kr.reshape(n_dec, 1, rope_dim))
            outs["mk_s"].append(mk.reshape(n_dec, 1, n_kv, HEAD_DIM))
            outs["mv_s"].append(mv.reshape(n_dec, 1, n_kv, HEAD_DIM))
        else:
            e_in = w_in_c.shape[2]
            w_in = _pad_cols(w_in_c[i], _round_up(e_in - IDX_DIM - n_idx + LANES, 2 * LANES)).astype(BF16)
            w_out = w_out_c[i].astype(BF16)
            o_k = n_dsa * HEAD_DIM
            offs = (0, o_k, o_k + HEAD_DIM, o_k + 2 * HEAD_DIM, o_k + 2 * HEAD_DIM + n_idx * IDX_DIM)
            q, k, v, qi, ki, wi = _post_odd(_mm(hn_p, w_in), tabs["p"]["head"], tabs["p"]["idx"], offs, n_idx)
            bias = _dsa_select_prompt(qi, wi, ki, batch, seq, min(DSA_TOPK, seq // 4))
            o = _dsa_prompt(q, k, v, bias, batch, seq, n_dsa)
            h_p = _mm(o, w_out, res=h_p)
            outs["dk_p"].append(k.reshape(batch, seq, HEAD_DIM))
            outs["dv_p"].append(v.reshape(batch, seq, HEAD_DIM))
            outs["dki_p"].append(ki.reshape(batch, seq, IDX_DIM))
            q, k, v, qi, ki, wi = _post_odd(_mm(hn_s, w_in), tabs["s"]["head"], tabs["s"]["idx"], offs, n_idx)
            scores = _dsa_score_decode(qi.astype(F32).reshape(n_dec, n_idx, IDX_DIM), wi.reshape(n_dec, n_idx, 1),
                                       ki.reshape(n_dec, 1, IDX_DIM), cache_ki_t, page_table, i)
            n_keys = past_len + 1
            bias = _dsa_select_decode(scores.reshape(n_dec, -1), n_keys, min(DSA_TOPK, n_keys // 4))
            o = _dsa_decode(q.astype(F32).reshape(n_dec, n_dsa, HEAD_DIM), k.reshape(n_dec, 1, HEAD_DIM),
                            v.reshape(n_dec, 1, HEAD_DIM), bias.reshape(n_dec, 1, -1),
                            cache_dsa_k, cache_dsa_v, page_table, i)
            h_s = _mm(o.reshape(n_dec, n_dsa * HEAD_DIM).astype(BF16), w_out, res=h_s)
            outs["dk_s"].append(k.reshape(n_dec, 1, HEAD_DIM))
            outs["dv_s"].append(v.reshape(n_dec, 1, HEAD_DIM))
            outs["dki_s"].append(ki.reshape(n_dec, 1, IDX_DIM))

        w_fi = w_ffn_in[layer].astype(BF16)
        w_fo = w_ffn_out[layer].astype(BF16)
        h_p, st_p = _ffn(h_p, _rmsnorm(h_p, norm_ffn[layer], BF16), w_fi, conv_w[layer], conv_b[layer], w_fo,
                         prompt_dims=(batch, seq))
        h_s, st_s = _ffn(h_s, _rmsnorm(h_s, norm_ffn[layer], BF16), w_fi, conv_w[layer], conv_b[layer], w_fo,
                         state=state_ffn_conv[layer])
        outs["cv_p"].append(st_p)
        outs["cv_s"].append(st_s)

    y_p = _rmsnorm(h_p, norm_final, F32).reshape(batch, seq, d_model)
    y_s = _rmsnorm(h_s, norm_final, F32).reshape(n_dec, 1, d_model)
    order = ("lat_p", "lat_s", "kr_p", "kr_s", "mk_p", "mk_s", "mv_p", "mv_s",
             "dk_p", "dk_s", "dv_p", "dv_s", "dki_p", "dki_s", "cv_p", "cv_s")
    return (y_p, y_s) + tuple(jnp.stack(outs[k]) for k in order)
```
